```python
import jax, jax.numpy as jnp
from jax import lax
import numpy as np

D_MODEL = 2048
BATCH = 4
SEQ = 4096
DEPTH = 2

D_MIX = D_MODEL
N_MIXERS = 4
GROUP_W = D_MIX // N_MIXERS
HEAD_DIM = 128
N_HEADS = GROUP_W // HEAD_DIM
POOL_WINDOWS = (2, 4, 8, 16)
SGU_CHUNK = 128
MOBA_BLOCK = 256
MOBA_TOPK = 3
MOBA_QCHUNK = 64
CONV_WIDTH = 3
D_FF = 11 * D_MODEL // 4
N_IN_COLS = 9 * GROUP_W
FFN_RES = 0.5
EPS = 1e-6

kernel_name = 'hybrid_pool_sgu_moba_conv_macaron'


def rms(x, g):
    xf = x.astype(jnp.float32)
    y = xf * lax.rsqrt(jnp.mean(xf * xf, axis=-1, keepdims=True) + EPS)
    return (y * g.astype(jnp.float32)).astype(x.dtype)


def swiglu(h, w13, w2):
    a, b = jnp.split(h @ w13, 2, axis=-1)
    return (jax.nn.silu(a) * b) @ w2


def pool_mixer(xa, w, scale):
    B, S, _ = xa.shape
    xg_all = xa.reshape(B, S, N_HEADS, HEAD_DIM)
    pos = jnp.arange(S)
    outs = []
    for g, win in enumerate(POOL_WINDOWS):
        xg = xg_all[:, :, g].astype(jnp.float32)
        cs = jnp.cumsum(xg, axis=1)
        lag = jnp.pad(cs[:, :S - win], ((0, 0), (win, 0), (0, 0)))
        cnt = jnp.minimum(pos + 1, win).astype(jnp.float32)[None, :, None]
        outs.append((cs - lag) / cnt - xg)
    d = jnp.stack(outs, axis=2).astype(xa.dtype)
    y = jnp.einsum('bsgc,gcd->bsgd', d, w).reshape(B, S, GROUP_W)
    return y * scale


def sgu_mixer(u, v, w_s, b_s, g_v):
    B, S, _ = u.shape
    u = jax.nn.gelu(u)
    v = jax.nn.gelu(v)
    vh = rms(v.reshape(B, S, N_HEADS, HEAD_DIM), g_v.reshape(N_HEADS, HEAD_DIM))
    vc = vh.reshape(B, S // SGU_CHUNK, SGU_CHUNK, N_HEADS, HEAD_DIM)
    causal = jnp.tril(jnp.ones((SGU_CHUNK, SGU_CHUNK), dtype=w_s.dtype))
    mixed = jnp.einsum('hts,bnshc->bnthc', w_s * causal[None], vc) + b_s.T[None, None, :, :, None]
    return u * mixed.reshape(B, S, GROUP_W)


def moba_mixer(q, k, v, gq, gk):
    B, S, _ = q.shape
    H, D, BLK, QC = N_HEADS, HEAD_DIM, MOBA_BLOCK, MOBA_QCHUNK

    def heads(t):
        return t.reshape(B, S, H, D).transpose(0, 2, 1, 3)

    qh = heads(rms(q.reshape(B, S, H, D), gq).reshape(B, S, GROUP_W)) * jnp.asarray(D ** -0.5, q.dtype)
    kh = heads(rms(k.reshape(B, S, H, D), gk).reshape(B, S, GROUP_W))
    vh = heads(v)
    nb = -(-S // BLK)
    pad = nb * BLK - S
    kb = jnp.pad(kh, ((0, 0), (0, 0), (0, pad), (0, 0))).reshape(B, H, nb, BLK, D)
    vb = jnp.pad(vh, ((0, 0), (0, 0), (0, pad), (0, 0))).reshape(B, H, nb, BLK, D)
    kmean = jnp.mean(kb.astype(jnp.float32), axis=3)
    topk = min(MOBA_TOPK, nb)
    bi = jnp.arange(B)[:, None, None, None]
    hi = jnp.arange(H)[None, :, None, None]
    blk_ids = jnp.arange(nb)
    kpos_local = jnp.arange(BLK)
    qpos_local = jnp.arange(QC)

    def chunk(n):
        q0 = n * QC
        own = q0 // BLK
        qc = lax.dynamic_slice_in_dim(qh, q0, QC, axis=2)
        s_blk = jnp.einsum('bhqd,bhnd->bhqn', qc.astype(jnp.float32), kmean)
        s_blk = jnp.where(blk_ids < own, s_blk, -jnp.inf)
        _, sel = lax.top_k(s_blk, topk)
        sel_ok = sel < own
        ks = kb[bi, hi, sel]
        vs = vb[bi, hi, sel]
        l_sel = jnp.einsum('bhqd,bhqkpd->bhqkp', qc, ks).astype(jnp.float32)
        l_sel = jnp.where(sel_ok[..., None], l_sel, -jnp.inf).reshape(B, H, QC, topk * BLK)
        ko = lax.dynamic_index_in_dim(kb, own, axis=2, keepdims=False)
        vo = lax.dynamic_index_in_dim(vb, own, axis=2, keepdims=False)
        l_own = jnp.einsum('bhqd,bhpd->bhqp', qc, ko).astype(jnp.float32)
        causal = (own * BLK + kpos_local)[None, :] <= (q0 + qpos_local)[:, None]
        l_own = jnp.where(causal, l_own, -jnp.inf)
        p = jax.nn.softmax(jnp.concatenate([l_sel, l_own], axis=-1), axis=-1).astype(vh.dtype)
        p_sel = p[..., :topk * BLK].reshape(B, H, QC, topk, BLK)
        p_own = p[..., topk * BLK:]
        return (jnp.einsum('bhqkp,bhqkpd->bhqd', p_sel, vs)
                + jnp.einsum('bhqp,bhpd->bhqd', p_own, vo))

    o = lax.map(chunk, jnp.arange(S // QC))
    return o.transpose(1, 0, 3, 2, 4).reshape(B, S, GROUP_W)


def conv_mixer(gb, gc, h, w):
    z = gc * h
    y = lax.conv_general_dilated(z, w[:, None, :], window_strides=(1,),
                                 padding=((CONV_WIDTH - 1, 0),),
                                 dimension_numbers=('NWC', 'WIO', 'NWC'),
                                 feature_group_count=GROUP_W)
    return gb * y


def setup_inputs(seed: int = 0) -> dict:
    key = jax.random.key(seed)
    ks = jax.random.split(key, 20)

    def nrm(k, shape, scale):
        return jax.random.normal(k, shape, jnp.float32) * scale

    L = DEPTH
    return {
        'x': nrm(ks[0], (BATCH, SEQ, D_MODEL), 1.0),
        'c': nrm(ks[1], (BATCH, D_MODEL), 1.0),
        'ada_w': nrm(ks[2], (L, D_MODEL, 9 * D_MODEL), 0.5 * D_MODEL ** -0.5),
        'ada_b': nrm(ks[3], (L, 9 * D_MODEL), 0.02),
        'norm_g': 1.0 + nrm(ks[4], (L, 3, D_MODEL), 0.1),
        'ffn1_w13': nrm(ks[5], (L, D_MODEL, 2 * D_FF), D_MODEL ** -0.5),
        'ffn1_w2': nrm(ks[6], (L, D_FF, D_MODEL), D_FF ** -0.5),
        'w_in': nrm(ks[7], (L, D_MODEL, N_IN_COLS), D_MODEL ** -0.5),
        'pool_w': nrm(ks[8], (L, N_HEADS, HEAD_DIM, HEAD_DIM), HEAD_DIM ** -0.5),
        'pool_scale': 1.0 + nrm(ks[9], (L, GROUP_W), 0.1),
        'sgu_w': nrm(ks[10], (L, N_HEADS, SGU_CHUNK, SGU_CHUNK), SGU_CHUNK ** -0.5),
        'sgu_b': 1.0 + nrm(ks[11], (L, N_HEADS, SGU_CHUNK), 0.1),
        'sgu_norm_g': 1.0 + nrm(ks[12], (L, GROUP_W), 0.1),
        'q_norm_g': 1.0 + nrm(ks[13], (L, HEAD_DIM), 0.1),
        'k_norm_g': 1.0 + nrm(ks[14], (L, HEAD_DIM), 0.1),
        'conv_w': nrm(ks[15], (L, CONV_WIDTH, GROUP_W), CONV_WIDTH ** -0.5),
        'out_norm_g': 1.0 + nrm(ks[16], (L, D_MIX), 0.1),
        'w_out': nrm(ks[17], (L, D_MIX, D_MODEL), D_MIX ** -0.5),
        'ffn2_w13': nrm(ks[18], (L, D_MODEL, 2 * D_FF), D_MODEL ** -0.5),
        'ffn2_w2': nrm(ks[19], (L, D_FF, D_MODEL), D_FF ** -0.5),
    }


def reference(x, c, ada_w, ada_b, norm_g, ffn1_w13, ffn1_w2, w_in, pool_w, pool_scale,
              sgu_w, sgu_b, sgu_norm_g, q_norm_g, k_norm_g, conv_w, out_norm_g, w_out,
              ffn2_w13, ffn2_w2):
    B, S, _ = x.shape
    for l in range(DEPTH):
        mod = (jax.nn.silu(c) @ ada_w[l] + ada_b[l]).reshape(B, 3, 3, 1, D_MODEL)

        def mod_norm(h, i):
            return rms(h, norm_g[l, i]) * (1.0 + mod[:, i, 1]) + mod[:, i, 0]

        x = x + FFN_RES * mod[:, 0, 2] * swiglu(mod_norm(x, 0), ffn1_w13[l], ffn1_w2[l])

        p = jnp.split(mod_norm(x, 1) @ w_in[l], 9, axis=-1)
        ya = pool_mixer(p[0], pool_w[l], pool_scale[l])
        yb = sgu_mixer(p[1], p[2], sgu_w[l], sgu_b[l], sgu_norm_g[l])
        yc = moba_mixer(p[3], p[4], p[5], q_norm_g[l], k_norm_g[l])
        yd = conv_mixer(p[6], p[7], p[8], conv_w[l])
        ycat = jnp.stack([ya, yb, yc, yd], axis=2)
        ycat = rms(ycat, out_norm_g[l].reshape(N_MIXERS, GROUP_W)).reshape(B, S, D_MIX)
        x = x + mod[:, 1, 2] * (ycat @ w_out[l])

        x = x + FFN_RES * mod[:, 2, 2] * swiglu(mod_norm(x, 2), ffn2_w13[l], ffn2_w2[l])
    return x
```

```python
import functools

import jax
import jax.numpy as jnp
from jax import lax
from jax.experimental import pallas as pl
from jax.experimental.pallas import tpu as pltpu

F32 = jnp.float32
BF16 = jnp.bfloat16

GROUP_W = 512
HEAD_DIM = 128
N_HEADS = GROUP_W // HEAD_DIM
N_GROUPS_IN = 9
POOL_WINDOWS = (2, 4, 8, 16)
POOL_HALO = 16
SGU_CHUNK = 128
MOBA_BLOCK = 256
MOBA_TOPK = 3
CONV_WIDTH = 3
CONV_HALO = 8
FFN_RES = 0.5
EPS = 1e-6
MOD_ROWS = 8

ADA_TN = 1024
FFN_TM = 512
FFN_TF = 512
OUT_CHUNK = 512
MIB = 1024 * 1024


def _rms(x, g):
    ms = jnp.mean(x * x, axis=-1, keepdims=True)
    return x * lax.rsqrt(ms + EPS) * g


def _silu(x):
    return x * (1.0 / (1.0 + jnp.exp(-x)))


def _gelu_tanh(x):
    c = 0.7978845608028654
    return x * (0.5 * (1.0 + jnp.tanh(c * (x + 0.044715 * (x * x * x)))))


def _dot(a, b):
    return jnp.dot(a, b, preferred_element_type=F32)


def _dot_nt(a, b, precision=None):
    return lax.dot_general(a, b, (((1,), (1,)), ((), ())),
                           precision=precision, preferred_element_type=F32)


def _ada_kernel(c_ref, w_ref, b_ref, o_ref):
    s = _silu(c_ref[...]).astype(BF16)
    o_ref[0] = _dot(s, w_ref[0].astype(BF16)) + b_ref[0]


def _ada_call(c_pad, ada_w, ada_b):
    n_layers, d, n_out = ada_w.shape
    return pl.pallas_call(
        _ada_kernel,
        out_shape=jax.ShapeDtypeStruct((n_layers, MOD_ROWS, n_out), F32),
        grid=(n_layers, n_out // ADA_TN),
        in_specs=[
            pl.BlockSpec((MOD_ROWS, d), lambda l, n: (0, 0)),
            pl.BlockSpec((1, d, ADA_TN), lambda l, n: (l, 0, n)),
            pl.BlockSpec((1, 1, ADA_TN), lambda l, n: (l, 0, n)),
        ],
        out_specs=pl.BlockSpec((1, MOD_ROWS, ADA_TN), lambda l, n: (l, 0, n)),
        compiler_params=pltpu.CompilerParams(
            dimension_semantics=("arbitrary", "arbitrary"),
            vmem_limit_bytes=32 * MIB),
        name="ada_mod",
    )(c_pad, ada_w, ada_b.reshape(n_layers, 1, n_out))


def _ffn_kernel(x_ref, mod_ref, g_ref, w1_ref, w3_ref, w2_ref, o_ref, h_ref, *, sub):
    j = pl.program_id(1)

    @pl.when(j == 0)
    def _():
        shift = mod_ref[0, 3 * sub:3 * sub + 1, :]
        scale = mod_ref[0, 3 * sub + 1:3 * sub + 2, :]
        h = _rms(x_ref[...], g_ref[...]) * (1.0 + scale) + shift
        h_ref[...] = h.astype(BF16)
        o_ref[...] = jnp.zeros_like(o_ref)

    h = h_ref[...]
    a = _dot(h, w1_ref[...])
    b = _dot(h, w3_ref[...])
    act = (_silu(a) * b).astype(BF16)
    d_out = o_ref.shape[1]
    for n in range(d_out // OUT_CHUNK):
        cs = slice(n * OUT_CHUNK, (n + 1) * OUT_CHUNK)
        o_ref[:, cs] += _dot(act, w2_ref[:, cs])

    @pl.when(j == pl.num_programs(1) - 1)
    def _():
        gate = mod_ref[0, 3 * sub + 2:3 * sub + 3, :]
        o_ref[...] = x_ref[...] + FFN_RES * gate * o_ref[...]


def _ffn_call(x2, mod_l, g, w13, w2, sub, seq):
    m, d = x2.shape
    d_ff = w2.shape[0]
    tiles_per_batch = seq // FFN_TM
    n_ff = d_ff // FFN_TF
    return pl.pallas_call(
        functools.partial(_ffn_kernel, sub=sub),
        out_shape=jax.ShapeDtypeStruct((m, d), F32),
        grid=(m // FFN_TM, n_ff),
        in_specs=[
            pl.BlockSpec((FFN_TM, d), lambda i, j: (i, 0)),
            pl.BlockSpec((1, 9, d), lambda i, j: (i // tiles_per_batch, 0, 0)),
            pl.BlockSpec((1, d), lambda i, j: (0, 0)),
            pl.BlockSpec((d, FFN_TF), lambda i, j: (0, j)),
            pl.BlockSpec((d, FFN_TF), lambda i, j: (0, j + n_ff)),
            pl.BlockSpec((FFN_TF, d), lambda i, j: (j, 0)),
        ],
        out_specs=pl.BlockSpec((FFN_TM, d), lambda i, j: (i, 0)),
        scratch_shapes=[pltpu.VMEM((FFN_TM, d), BF16)],
        compiler_params=pltpu.CompilerParams(
            dimension_semantics=("arbitrary", "arbitrary"),
            vmem_limit_bytes=48 * MIB),
        name=f"ffn{sub}",
    )(x2, mod_l, g, w13, w13, w2)


def _in_mix_kernel(x_ref, mod_ref, g_ref, win_ref, poolw_ref, pools_ref, sguw_ref,
                   sgubt_ref, sgug_ref, gq_ref, gk_ref, convw_ref, ong_ref,
                   yabd_ref, q_ref, k_ref, vt_ref, kmean_ref,
                   pbuf, zbuf, *, tiles_per_batch):
    tm = x_ref.shape[0]
    tib = pl.program_id(0) % tiles_per_batch

    @pl.when(tib == 0)
    def _():
        pbuf[0:POOL_HALO, :] = jnp.zeros((POOL_HALO, GROUP_W), F32)
        zbuf[0:CONV_HALO, :] = jnp.zeros((CONV_HALO, GROUP_W), F32)

    shift = mod_ref[0, 3:4, :]
    scale = mod_ref[0, 4:5, :]
    h = (_rms(x_ref[...], g_ref[...]) * (1.0 + scale) + shift).astype(BF16)

    def proj(g):
        return _dot(h, win_ref[:, g * GROUP_W:(g + 1) * GROUP_W])

    def heads():
        return [slice(hd * HEAD_DIM, (hd + 1) * HEAD_DIM) for hd in range(N_HEADS)]

    pbuf[POOL_HALO:POOL_HALO + tm, :] = proj(0)
    pos = tib * tm + lax.broadcasted_iota(jnp.int32, (tm, 1), 0)
    ya_parts = []
    for hd, (cs, win) in enumerate(zip(heads(), POOL_WINDOWS)):
        xg = pbuf[POOL_HALO:POOL_HALO + tm, cs]
        wsum = xg
        for lag in range(1, win):
            wsum = wsum + pbuf[POOL_HALO - lag:POOL_HALO - lag + tm, cs]
        cnt = jnp.minimum(pos + 1, win).astype(F32)
        dlt = wsum / cnt - xg
        ya_parts.append(_dot(dlt.astype(BF16), poolw_ref[hd]))
    ya = jnp.concatenate(ya_parts, axis=1) * pools_ref[...]
    pbuf[0:POOL_HALO, :] = pbuf[tm:tm + POOL_HALO, :]
    yabd_ref[:, 0:GROUP_W] = _rms(ya, ong_ref[0:1, :]).astype(BF16)

    u = _gelu_tanh(proj(1))
    v = _gelu_tanh(proj(2))
    r_t = lax.broadcasted_iota(jnp.int32, (SGU_CHUNK, SGU_CHUNK), 0)
    r_s = lax.broadcasted_iota(jnp.int32, (SGU_CHUNK, SGU_CHUNK), 1)
    yb_parts = []
    for hd, cs in enumerate(heads()):
        vh = _rms(v[:, cs], sgug_ref[:, cs]).astype(BF16)
        wm = jnp.where(r_t >= r_s, sguw_ref[hd], 0.0).astype(BF16)
        bias = sgubt_ref[:, hd:hd + 1]
        mixed = [_dot(wm, vh[c * SGU_CHUNK:(c + 1) * SGU_CHUNK, :]) + bias
                 for c in range(tm // SGU_CHUNK)]
        yb_parts.append(u[:, cs] * jnp.concatenate(mixed, axis=0))
    yb = jnp.concatenate(yb_parts, axis=1)
    yabd_ref[:, GROUP_W:2 * GROUP_W] = _rms(yb, ong_ref[1:2, :]).astype(BF16)

    gate_b = proj(6)
    zbuf[CONV_HALO:CONV_HALO + tm, :] = proj(7) * proj(8)
    conv = jnp.zeros((tm, GROUP_W), F32)
    for tap in range(CONV_WIDTH):
        off = CONV_HALO - (CONV_WIDTH - 1) + tap
        conv = conv + convw_ref[tap:tap + 1, :] * zbuf[off:off + tm, :]
    zbuf[0:CONV_HALO, :] = zbuf[tm:tm + CONV_HALO, :]
    yabd_ref[:, 2 * GROUP_W:3 * GROUP_W] = _rms(gate_b * conv, ong_ref[3:4, :]).astype(BF16)

    q_raw = proj(3)
    k_raw = proj(4)
    qk_scale = HEAD_DIM ** -0.5
    q = jnp.concatenate([_rms(q_raw[:, cs], gq_ref[...]) * qk_scale for cs in heads()], axis=1)
    k = jnp.concatenate([_rms(k_raw[:, cs], gk_ref[...]) for cs in heads()], axis=1)
    q_ref[...] = q
    k_ref[...] = k.astype(BF16)
    kmean_ref[0] = jnp.mean(k, axis=0, keepdims=True)
    vt_ref[0] = proj(5).T.astype(BF16)


def _in_mix_call(x2, mod_l, g, w_in, pool_w, pool_scale, sgu_w, sgu_bt, sgu_g,
                 gq, gk, conv_w, out_g, seq):
    m, d = x2.shape
    tm = MOBA_BLOCK
    n_tiles = m // tm
    tiles_per_batch = seq // tm
    const2 = lambda i: (0, 0)
    const3 = lambda i: (0, 0, 0)
    return pl.pallas_call(
        functools.partial(_in_mix_kernel, tiles_per_batch=tiles_per_batch),
        out_shape=(
            jax.ShapeDtypeStruct((m, 3 * GROUP_W), BF16),
            jax.ShapeDtypeStruct((m, GROUP_W), F32),
            jax.ShapeDtypeStruct((m, GROUP_W), BF16),
            jax.ShapeDtypeStruct((n_tiles, GROUP_W, tm), BF16),
            jax.ShapeDtypeStruct((n_tiles, 1, GROUP_W), F32),
        ),
        grid=(n_tiles,),
        in_specs=[
            pl.BlockSpec((tm, d), lambda i: (i, 0)),
            pl.BlockSpec((1, 9, d), lambda i: (i // tiles_per_batch, 0, 0)),
            pl.BlockSpec((1, d), const2),
            pl.BlockSpec(w_in.shape, const2),
            pl.BlockSpec(pool_w.shape, const3),
            pl.BlockSpec((1, GROUP_W), const2),
            pl.BlockSpec(sgu_w.shape, const3),
            pl.BlockSpec(sgu_bt.shape, const2),
            pl.BlockSpec((1, GROUP_W), const2),
            pl.BlockSpec((1, HEAD_DIM), const2),
            pl.BlockSpec((1, HEAD_DIM), const2),
            pl.BlockSpec(conv_w.shape, const2),
            pl.BlockSpec(out_g.shape, const2),
        ],
        out_specs=(
            pl.BlockSpec((tm, 3 * GROUP_W), lambda i: (i, 0)),
            pl.BlockSpec((tm, GROUP_W), lambda i: (i, 0)),
            pl.BlockSpec((tm, GROUP_W), lambda i: (i, 0)),
            pl.BlockSpec((1, GROUP_W, tm), lambda i: (i, 0, 0)),
            pl.BlockSpec((1, 1, GROUP_W), lambda i: (i, 0, 0)),
        ),
        scratch_shapes=[
            pltpu.VMEM((POOL_HALO + tm, GROUP_W), F32),
            pltpu.VMEM((CONV_HALO + tm, GROUP_W), F32),
        ],
        compiler_params=pltpu.CompilerParams(
            dimension_semantics=("arbitrary",),
            vmem_limit_bytes=56 * MIB),
        name="in_mix",
    )(x2, mod_l, g, w_in, pool_w, pool_scale, sgu_w, sgu_bt, sgu_g, gq, gk, conv_w, out_g)


def _moba_out_kernel(q_ref, k_ref, vt_ref, kmean_ref, yabd_ref, x_ref, mod_ref, ong_ref,
                     wout_ref, o_ref, ycat_ref, sel_ref):
    own = pl.program_id(1)
    tq = q_ref.shape[0]
    n_blk = kmean_ref.shape[1]
    neg_inf = -jnp.inf
    blk_id = lax.broadcasted_iota(jnp.int32, (n_blk, tq), 0).astype(F32)
    past = blk_id < own.astype(F32)
    key_i = lax.broadcasted_iota(jnp.int32, (tq, tq), 0)
    qry_i = lax.broadcasted_iota(jnp.int32, (tq, tq), 1)

    yc_parts = []
    for hd in range(N_HEADS):
        cs = slice(hd * HEAD_DIM, (hd + 1) * HEAD_DIM)
        q_f32 = q_ref[:, cs]
        q_bf = q_f32.astype(BF16)

        sb = _dot_nt(kmean_ref[0, :, cs], q_f32, precision=lax.Precision.HIGHEST)
        sb = jnp.where(past, sb, neg_inf)
        sel = jnp.zeros((n_blk, tq), F32)
        for _ in range(min(MOBA_TOPK, n_blk)):
            top = jnp.max(sb, axis=0, keepdims=True)
            idx = jnp.min(jnp.where(sb == top, blk_id, float(n_blk)), axis=0, keepdims=True)
            hit = blk_id == idx
            sel = jnp.where(hit, 1.0, sel)
            sb = jnp.where(hit, neg_inf, sb)
        sel = jnp.where(past, sel, 0.0)
        for nb in range(n_blk):
            sel_ref[nb] = sel[nb:nb + 1, :]

        s = _dot_nt(k_ref[0, own, :, cs], q_bf)
        s = jnp.where(key_i <= qry_i, s, neg_inf)
        m0 = jnp.max(s, axis=0, keepdims=True)
        p = jnp.exp(s - m0)
        l0 = jnp.sum(p, axis=0, keepdims=True)
        acc0 = _dot(vt_ref[0, own, cs, :], p.astype(BF16))

        def body(nb, carry):
            m, l, acc = carry
            s = _dot_nt(k_ref[0, nb, :, cs], q_bf)
            s = jnp.where(sel_ref[nb] > 0.0, s, neg_inf)
            m_new = jnp.maximum(m, jnp.max(s, axis=0, keepdims=True))
            alpha = jnp.exp(m - m_new)
            p = jnp.exp(s - m_new)
            l = alpha * l + jnp.sum(p, axis=0, keepdims=True)
            acc = alpha * acc + _dot(vt_ref[0, nb, cs, :], p.astype(BF16))
            return m_new, l, acc

        _, l, acc = lax.fori_loop(0, own, body, (m0, l0, acc0))
        yc_parts.append((acc / l).T)

    yc = jnp.concatenate(yc_parts, axis=1)
    ycat_ref[:, 0:2 * GROUP_W] = yabd_ref[:, 0:2 * GROUP_W]
    ycat_ref[:, 2 * GROUP_W:3 * GROUP_W] = _rms(yc, ong_ref[2:3, :]).astype(BF16)
    ycat_ref[:, 3 * GROUP_W:4 * GROUP_W] = yabd_ref[:, 2 * GROUP_W:3 * GROUP_W]
    gate = mod_ref[0, 5:6, :]
    o_ref[...] = x_ref[...] + gate * _dot(ycat_ref[...], wout_ref[...])


def _moba_out_call(q, k4, vt4, kmean, yabd, x2, mod_l, out_g, w_out):
    m, d = x2.shape
    n_batch, n_blk, tq, _ = k4.shape
    row = lambda b, j: (b * n_blk + j, 0)
    return pl.pallas_call(
        _moba_out_kernel,
        out_shape=jax.ShapeDtypeStruct((m, d), F32),
        grid=(n_batch, n_blk),
        in_specs=[
            pl.BlockSpec((tq, GROUP_W), row),
            pl.BlockSpec((1, n_blk, tq, GROUP_W), lambda b, j: (b, 0, 0, 0)),
            pl.BlockSpec((1, n_blk, GROUP_W, tq), lambda b, j: (b, 0, 0, 0)),
            pl.BlockSpec((1, n_blk, GROUP_W), lambda b, j: (b, 0, 0)),
            pl.BlockSpec((tq, 3 * GROUP_W), row),
            pl.BlockSpec((tq, d), row),
            pl.BlockSpec((1, 9, d), lambda b, j: (b, 0, 0)),
            pl.BlockSpec(out_g.shape, lambda b, j: (0, 0)),
            pl.BlockSpec(w_out.shape, lambda b, j: (0, 0)),
        ],
        out_specs=pl.BlockSpec((tq, d), row),
        scratch_shapes=[
            pltpu.VMEM((tq, 4 * GROUP_W), BF16),
            pltpu.VMEM((n_blk, 1, tq), F32),
        ],
        compiler_params=pltpu.CompilerParams(
            dimension_semantics=("arbitrary", "arbitrary"),
            vmem_limit_bytes=56 * MIB),
        name="moba_out",
    )(q, k4, vt4, kmean, yabd, x2, mod_l, out_g, w_out)


def kernel(x, c, ada_w, ada_b, norm_g, ffn1_w13, ffn1_w2, w_in, pool_w, pool_scale, sgu_w,
           sgu_b, sgu_norm_g, q_norm_g, k_norm_g, conv_w, out_norm_g, w_out, ffn2_w13, ffn2_w2):
    n_batch, seq, d = x.shape
    n_layers = ada_w.shape[0]
    assert seq % MOBA_BLOCK == 0 and seq % FFN_TM == 0 and MOBA_BLOCK % SGU_CHUNK == 0
    assert n_batch <= MOD_ROWS and w_in.shape[2] == N_GROUPS_IN * GROUP_W
    n_blk = seq // MOBA_BLOCK

    c_pad = jnp.zeros((MOD_ROWS, d), F32).at[:n_batch].set(c)
    mod = _ada_call(c_pad, ada_w, ada_b)[:, :n_batch].reshape(n_layers, n_batch, 9, d)

    x2 = x.reshape(n_batch * seq, d)
    for l in range(n_layers):
        mod_l = mod[l]
        x2 = _ffn_call(x2, mod_l, norm_g[l, 0:1], ffn1_w13[l].astype(BF16),
                       ffn1_w2[l].astype(BF16), 0, seq)
        yabd, q, k, vt, kmean = _in_mix_call(
            x2, mod_l, norm_g[l, 1:2], w_in[l].astype(BF16), pool_w[l].astype(BF16),
            pool_scale[l].reshape(1, GROUP_W), sgu_w[l], sgu_b[l].T,
            sgu_norm_g[l].reshape(1, GROUP_W), q_norm_g[l].reshape(1, HEAD_DIM),
            k_norm_g[l].reshape(1, HEAD_DIM), conv_w[l],
            out_norm_g[l].reshape(4, GROUP_W), seq)
        x2 = _moba_out_call(
            q, k.reshape(n_batch, n_blk, MOBA_BLOCK, GROUP_W),
            vt.reshape(n_batch, n_blk, GROUP_W, MOBA_BLOCK),
            kmean.reshape(n_batch, n_blk, GROUP_W), yabd, x2, mod_l,
            out_norm_g[l].reshape(4, GROUP_W), w_out[l].astype(BF16))
        x2 = _ffn_call(x2, mod_l, norm_g[l, 2:3], ffn2_w13[l].astype(BF16),
                       ffn2_w2[l].astype(BF16), 2, seq)
    return x2.reshape(n_batch, seq, d)
```

```python
import functools

import jax
import jax.numpy as jnp
from jax import lax
from jax.experimental import pallas as pl
from jax.experimental.pallas import tpu as pltpu

F32 = jnp.float32
BF16 = jnp.bfloat16

GROUP_W = 512
HEAD_DIM = 128
N_HEADS = GROUP_W // HEAD_DIM
N_GROUPS_IN = 9
POOL_WINDOWS = (2, 4, 8, 16)
POOL_HALO = 16
SGU_CHUNK = 128
MOBA_BLOCK = 256
MOBA_TOPK = 3
CONV_WIDTH = 3
CONV_HALO = 8
FFN_RES = 0.5
EPS = 1e-6
MOD_ROWS = 8

ADA_TN = 1024
MIB = 1024 * 1024
FFN_TM = 1024
FFN_SUB_ROWS = 512
FFN_TF = 512
OUT_CHUNK = 512
NORM_ROWS = 16
NORM_UNROLL = 8
FFN_VMEM_LIMIT = (2 * 2 * FFN_TM * 2048 * 4 + FFN_TM * 2048 * 2 + 2 * 3 * 2048 * FFN_TF * 2
                  + FFN_SUB_ROWS * (3 * FFN_TF + OUT_CHUNK) * 4 + 4 * MIB)


def _rms(x, g):
    ms = jnp.mean(x * x, axis=-1, keepdims=True)
    return x * lax.rsqrt(ms + EPS) * g


def _silu(x):
    return x * (1.0 / (1.0 + jnp.exp(-x)))


def _gelu_tanh(x):
    c = 0.7978845608028654
    return x * (0.5 * (1.0 + jnp.tanh(c * (x + 0.044715 * (x * x * x)))))


def _dot(a, b):
    return jnp.dot(a, b, preferred_element_type=F32)


def _dot_nt(a, b, precision=None):
    return lax.dot_general(a, b, (((1,), (1,)), ((), ())),
                           precision=precision, preferred_element_type=F32)


def _ada_kernel(c_ref, w_ref, b_ref, o_ref):
    s = _silu(c_ref[...]).astype(BF16)
    o_ref[0] = _dot(s, w_ref[0].astype(BF16)) + b_ref[0]


def _ada_call(c_pad, ada_w, ada_b):
    n_layers, d, n_out = ada_w.shape
    return pl.pallas_call(
        _ada_kernel,
        out_shape=jax.ShapeDtypeStruct((n_layers, MOD_ROWS, n_out), F32),
        grid=(n_layers, n_out // ADA_TN),
        in_specs=[
            pl.BlockSpec((MOD_ROWS, d), lambda l, n: (0, 0)),
            pl.BlockSpec((1, d, ADA_TN), lambda l, n: (l, 0, n)),
            pl.BlockSpec((1, 1, ADA_TN), lambda l, n: (l, 0, n)),
        ],
        out_specs=pl.BlockSpec((1, MOD_ROWS, ADA_TN), lambda l, n: (l, 0, n)),
        compiler_params=pltpu.CompilerParams(
            dimension_semantics=("arbitrary", "arbitrary"),
            vmem_limit_bytes=32 * MIB),
        name="ada_mod",
    )(c_pad, ada_w, ada_b.reshape(n_layers, 1, n_out))


def _ffn_kernel(x_ref, mod_ref, g_ref, w1_ref, w3_ref, w2_ref, o_ref, h_ref, *, sub):
    j = pl.program_id(1)
    tm, d_out = o_ref.shape

    @pl.when(j == 0)
    def _():
        shift = mod_ref[0, 3 * sub:3 * sub + 1, :]
        gain = g_ref[...] * (1.0 + mod_ref[0, 3 * sub + 1:3 * sub + 2, :])

        def norm_rows(r, carry):
            rows = pl.ds(pl.multiple_of(r * NORM_ROWS, NORM_ROWS), NORM_ROWS)
            x = x_ref[rows, :]
            inv = lax.rsqrt(jnp.mean(x * x, axis=-1, keepdims=True) + EPS)
            h_ref[rows, :] = (x * inv * gain + shift).astype(BF16)
            o_ref[rows, :] = jnp.zeros((NORM_ROWS, d_out), F32)
            return carry

        lax.fori_loop(0, tm // NORM_ROWS, norm_rows, 0, unroll=NORM_UNROLL)

    for r in range(tm // FFN_SUB_ROWS):
        rows = slice(r * FFN_SUB_ROWS, (r + 1) * FFN_SUB_ROWS)
        h = h_ref[rows, :]
        a = _dot(h, w1_ref[...])
        b = _dot(h, w3_ref[...])
        act = (_silu(a) * b).astype(BF16)
        for n in range(d_out // OUT_CHUNK):
            cs = slice(n * OUT_CHUNK, (n + 1) * OUT_CHUNK)
            o_ref[rows, cs] += _dot(act, w2_ref[:, cs])

    @pl.when(j == pl.num_programs(1) - 1)
    def _():
        gate = FFN_RES * mod_ref[0, 3 * sub + 2:3 * sub + 3, :]

        def residual_rows(r, carry):
            rows = pl.ds(pl.multiple_of(r * NORM_ROWS, NORM_ROWS), NORM_ROWS)
            o_ref[rows, :] = x_ref[rows, :] + gate * o_ref[rows, :]
            return carry

        lax.fori_loop(0, tm // NORM_ROWS, residual_rows, 0, unroll=NORM_UNROLL)


def _ffn_call(x2, mod_l, g, w13, w2, layer, sub, seq):
    m, d = x2.shape
    d_ff = w2.shape[1]
    tiles_per_batch = seq // FFN_TM
    n_ff = d_ff // FFN_TF
    return pl.pallas_call(
        functools.partial(_ffn_kernel, sub=sub),
        out_shape=jax.ShapeDtypeStruct((m, d), F32),
        grid=(m // FFN_TM, n_ff),
        in_specs=[
            pl.BlockSpec((FFN_TM, d), lambda i, j: (i, 0)),
            pl.BlockSpec((1, 9, d), lambda i, j: (i // tiles_per_batch, 0, 0)),
            pl.BlockSpec((1, d), lambda i, j: (0, 0)),
            pl.BlockSpec((None, d, FFN_TF), lambda i, j: (layer, 0, j)),
            pl.BlockSpec((None, d, FFN_TF), lambda i, j: (layer, 0, j + n_ff)),
            pl.BlockSpec((None, FFN_TF, d), lambda i, j: (layer, j, 0)),
        ],
        out_specs=pl.BlockSpec((FFN_TM, d), lambda i, j: (i, 0)),
        scratch_shapes=[pltpu.VMEM((FFN_TM, d), BF16)],
        compiler_params=pltpu.CompilerParams(
            dimension_semantics=("arbitrary", "arbitrary"),
            vmem_limit_bytes=FFN_VMEM_LIMIT),
        name=f"ffn{sub}",
    )(x2, mod_l, g, w13, w13, w2)


def _in_mix_kernel(x_ref, mod_ref, g_ref, win_ref, poolw_ref, pools_ref, sguw_ref,
                   sgubt_ref, sgug_ref, gq_ref, gk_ref, convw_ref, ong_ref,
                   yabd_ref, q_ref, k_ref, vt_ref, kmean_ref,
                   pbuf, zbuf, *, tiles_per_batch):
    tm = x_ref.shape[0]
    tib = pl.program_id(0) % tiles_per_batch

    @pl.when(tib == 0)
    def _():
        pbuf[0:POOL_HALO, :] = jnp.zeros((POOL_HALO, GROUP_W), F32)
        zbuf[0:CONV_HALO, :] = jnp.zeros((CONV_HALO, GROUP_W), F32)

    shift = mod_ref[0, 3:4, :]
    scale = mod_ref[0, 4:5, :]
    h = (_rms(x_ref[...], g_ref[...]) * (1.0 + scale) + shift).astype(BF16)

    def proj(g):
        return _dot(h, win_ref[:, g * GROUP_W:(g + 1) * GROUP_W])

    def heads():
        return [slice(hd * HEAD_DIM, (hd + 1) * HEAD_DIM) for hd in range(N_HEADS)]

    pbuf[POOL_HALO:POOL_HALO + tm, :] = proj(0)
    pos = tib * tm + lax.broadcasted_iota(jnp.int32, (tm, 1), 0)
    ya_parts = []
    for hd, (cs, win) in enumerate(zip(heads(), POOL_WINDOWS)):
        xg = pbuf[POOL_HALO:POOL_HALO + tm, cs]
        wsum = xg
        for lag in range(1, win):
            wsum = wsum + pbuf[POOL_HALO - lag:POOL_HALO - lag + tm, cs]
        cnt = jnp.minimum(pos + 1, win).astype(F32)
        dlt = wsum / cnt - xg
        ya_parts.append(_dot(dlt.astype(BF16), poolw_ref[hd]))
    ya = jnp.concatenate(ya_parts, axis=1) * pools_ref[...]
    pbuf[0:POOL_HALO, :] = pbuf[tm:tm + POOL_HALO, :]
    yabd_ref[:, 0:GROUP_W] = _rms(ya, ong_ref[0:1, :]).astype(BF16)

    u = _gelu_tanh(proj(1))
    v = _gelu_tanh(proj(2))
    r_t = lax.broadcasted_iota(jnp.int32, (SGU_CHUNK, SGU_CHUNK), 0)
    r_s = lax.broadcasted_iota(jnp.int32, (SGU_CHUNK, SGU_CHUNK), 1)
    yb_parts = []
    for hd, cs in enumerate(heads()):
        vh = _rms(v[:, cs], sgug_ref[:, cs]).astype(BF16)
        wm = jnp.where(r_t >= r_s, sguw_ref[hd], 0.0).astype(BF16)
        bias = sgubt_ref[:, hd:hd + 1]
        mixed = [_dot(wm, vh[c * SGU_CHUNK:(c + 1) * SGU_CHUNK, :]) + bias
                 for c in range(tm // SGU_CHUNK)]
        yb_parts.append(u[:, cs] * jnp.concatenate(mixed, axis=0))
    yb = jnp.concatenate(yb_parts, axis=1)
    yabd_ref[:, GROUP_W:2 * GROUP_W] = _rms(yb, ong_ref[1:2, :]).astype(BF16)

    gate_b = proj(6)
    zbuf[CONV_HALO:CONV_HALO + tm, :] = proj(7) * proj(8)
    conv = jnp.zeros((tm, GROUP_W), F32)
    for tap in range(CONV_WIDTH):
        off = CONV_HALO - (CONV_WIDTH - 1) + tap
        conv = conv + convw_ref[tap:tap + 1, :] * zbuf[off:off + tm, :]
    zbuf[0:CONV_HALO, :] = zbuf[tm:tm + CONV_HALO, :]
    yabd_ref[:, 2 * GROUP_W:3 * GROUP_W] = _rms(gate_b * conv, ong_ref[3:4, :]).astype(BF16)

    q_raw = proj(3)
    k_raw = proj(4)
    qk_scale = HEAD_DIM ** -0.5
    q = jnp.concatenate([_rms(q_raw[:, cs], gq_ref[...]) * qk_scale for cs in heads()], axis=1)
    k = jnp.concatenate([_rms(k_raw[:, cs], gk_ref[...]) for cs in heads()], axis=1)
    q_ref[...] = q
    k_ref[...] = k.astype(BF16)
    kmean_ref[0] = jnp.mean(k, axis=0, keepdims=True)
    vt_ref[0] = proj(5).T.astype(BF16)


def _in_mix_call(x2, mod_l, g, w_in, pool_w, pool_scale, sgu_w, sgu_bt, sgu_g,
                 gq, gk, conv_w, out_g, layer, seq):
    m, d = x2.shape
    tm = MOBA_BLOCK
    n_tiles = m // tm
    tiles_per_batch = seq // tm
    const2 = lambda i: (0, 0)
    const3 = lambda i: (0, 0, 0)
    return pl.pallas_call(
        functools.partial(_in_mix_kernel, tiles_per_batch=tiles_per_batch),
        out_shape=(
            jax.ShapeDtypeStruct((m, 3 * GROUP_W), BF16),
            jax.ShapeDtypeStruct((m, GROUP_W), F32),
            jax.ShapeDtypeStruct((m, GROUP_W), BF16),
            jax.ShapeDtypeStruct((n_tiles, GROUP_W, tm), BF16),
            jax.ShapeDtypeStruct((n_tiles, 1, GROUP_W), F32),
        ),
        grid=(n_tiles,),
        in_specs=[
            pl.BlockSpec((tm, d), lambda i: (i, 0)),
            pl.BlockSpec((1, 9, d), lambda i: (i // tiles_per_batch, 0, 0)),
            pl.BlockSpec((1, d), const2),
            pl.BlockSpec((None,) + w_in.shape[1:], lambda i: (layer, 0, 0)),
            pl.BlockSpec((None,) + pool_w.shape[1:], lambda i: (layer, 0, 0, 0)),
            pl.BlockSpec((1, GROUP_W), const2),
            pl.BlockSpec(sgu_w.shape, const3),
            pl.BlockSpec(sgu_bt.shape, const2),
            pl.BlockSpec((1, GROUP_W), const2),
            pl.BlockSpec((1, HEAD_DIM), const2),
            pl.BlockSpec((1, HEAD_DIM), const2),
            pl.BlockSpec(conv_w.shape, const2),
            pl.BlockSpec(out_g.shape, const2),
        ],
        out_specs=(
            pl.BlockSpec((tm, 3 * GROUP_W), lambda i: (i, 0)),
            pl.BlockSpec((tm, GROUP_W), lambda i: (i, 0)),
            pl.BlockSpec((tm, GROUP_W), lambda i: (i, 0)),
            pl.BlockSpec((1, GROUP_W, tm), lambda i: (i, 0, 0)),
            pl.BlockSpec((1, 1, GROUP_W), lambda i: (i, 0, 0)),
        ),
        scratch_shapes=[
            pltpu.VMEM((POOL_HALO + tm, GROUP_W), F32),
            pltpu.VMEM((CONV_HALO + tm, GROUP_W), F32),
        ],
        compiler_params=pltpu.CompilerParams(
            dimension_semantics=("arbitrary",),
            vmem_limit_bytes=56 * MIB),
        name="in_mix",
    )(x2, mod_l, g, w_in, pool_w, pool_scale, sgu_w, sgu_bt, sgu_g, gq, gk, conv_w, out_g)


def _moba_out_kernel(q_ref, k_ref, vt_ref, kmean_ref, yabd_ref, x_ref, mod_ref, ong_ref,
                     wout_ref, o_ref, ycat_ref, qb_ref, sel_ref, m_ref, l_ref, acc_ref):
    own = pl.program_id(1)
    tq = q_ref.shape[0]
    n_blk = kmean_ref.shape[1]
    neg_inf = -jnp.inf
    blk_id = lax.broadcasted_iota(jnp.int32, (n_blk, tq), 0).astype(F32)
    past = blk_id < own.astype(F32)
    key_i = lax.broadcasted_iota(jnp.int32, (tq, tq), 0)
    qry_i = lax.broadcasted_iota(jnp.int32, (tq, tq), 1)
    heads = [slice(hd * HEAD_DIM, (hd + 1) * HEAD_DIM) for hd in range(N_HEADS)]

    qb_ref[...] = q_ref[...].astype(BF16)
    blk_scores = [_dot_nt(kmean_ref[0, :, cs], q_ref[:, cs], precision=lax.Precision.HIGHEST)
                  for cs in heads]
    own_scores = [_dot_nt(k_ref[0, own, :, cs], qb_ref[:, cs]) for cs in heads]
    for hd, cs in enumerate(heads):
        sb = jnp.where(past, blk_scores[hd], neg_inf)
        sel = jnp.zeros((n_blk, tq), F32)
        for _ in range(min(MOBA_TOPK, n_blk)):
            top = jnp.max(sb, axis=0, keepdims=True)
            idx = jnp.min(jnp.where(sb == top, blk_id, float(n_blk)), axis=0, keepdims=True)
            hit = blk_id == idx
            sel = jnp.where(hit, 1.0, sel)
            sb = jnp.where(hit, neg_inf, sb)
        sel = jnp.where(past, sel, 0.0)
        for nb in range(n_blk):
            sel_ref[hd, nb] = sel[nb:nb + 1, :]

        s = jnp.where(key_i <= qry_i, own_scores[hd], neg_inf)
        m0 = jnp.max(s, axis=0, keepdims=True)
        p = jnp.exp(s - m0)
        m_ref[hd] = m0
        l_ref[hd] = jnp.sum(p, axis=0, keepdims=True)
        acc_ref[hd] = _dot(vt_ref[0, own, cs, :], p.astype(BF16))

    def past_block(nb, carry):
        scores = [_dot_nt(k_ref[0, nb, :, cs], qb_ref[:, cs]) for cs in heads]
        for hd, cs in enumerate(heads):
            s = jnp.where(sel_ref[hd, nb] > 0.0, scores[hd], neg_inf)
            m_old = m_ref[hd]
            m_new = jnp.maximum(m_old, jnp.max(s, axis=0, keepdims=True))
            alpha = jnp.exp(m_old - m_new)
            p = jnp.exp(s - m_new)
            m_ref[hd] = m_new
            l_ref[hd] = alpha * l_ref[hd] + jnp.sum(p, axis=0, keepdims=True)
            acc_ref[hd] = alpha * acc_ref[hd] + _dot(vt_ref[0, nb, cs, :], p.astype(BF16))
        return carry

    lax.fori_loop(0, own, past_block, 0)

    yc = jnp.concatenate([(acc_ref[hd] / l_ref[hd]).T for hd in range(N_HEADS)], axis=1)
    ycat_ref[:, 0:2 * GROUP_W] = yabd_ref[:, 0:2 * GROUP_W]
    ycat_ref[:, 2 * GROUP_W:3 * GROUP_W] = _rms(yc, ong_ref[2:3, :]).astype(BF16)
    ycat_ref[:, 3 * GROUP_W:4 * GROUP_W] = yabd_ref[:, 2 * GROUP_W:3 * GROUP_W]
    gate = mod_ref[0, 5:6, :]
    o_ref[...] = x_ref[...] + gate * _dot(ycat_ref[...], wout_ref[...])


def _moba_out_call(q, k4, vt4, kmean, yabd, x2, mod_l, out_g, w_out, layer):
    m, d = x2.shape
    n_batch, n_blk, tq, _ = k4.shape
    row = lambda b, j: (b * n_blk + j, 0)
    return pl.pallas_call(
        _moba_out_kernel,
        out_shape=jax.ShapeDtypeStruct((m, d), F32),
        grid=(n_batch, n_blk),
        in_specs=[
            pl.BlockSpec((tq, GROUP_W), row),
            pl.BlockSpec((1, n_blk, tq, GROUP_W), lambda b, j: (b, 0, 0, 0)),
            pl.BlockSpec((1, n_blk, GROUP_W, tq), lambda b, j: (b, 0, 0, 0)),
            pl.BlockSpec((1, n_blk, GROUP_W), lambda b, j: (b, 0, 0)),
            pl.BlockSpec((tq, 3 * GROUP_W), row),
            pl.BlockSpec((tq, d), row),
            pl.BlockSpec((1, 9, d), lambda b, j: (b, 0, 0)),
            pl.BlockSpec(out_g.shape, lambda b, j: (0, 0)),
            pl.BlockSpec((None,) + w_out.shape[1:], lambda b, j: (layer, 0, 0)),
        ],
        out_specs=pl.BlockSpec((tq, d), row),
        scratch_shapes=[
            pltpu.VMEM((tq, 4 * GROUP_W), BF16),
            pltpu.VMEM((tq, GROUP_W), BF16),
            pltpu.VMEM((N_HEADS, n_blk, 1, tq), F32),
            pltpu.VMEM((N_HEADS, 1, tq), F32),
            pltpu.VMEM((N_HEADS, 1, tq), F32),
            pltpu.VMEM((N_HEADS, HEAD_DIM, tq), F32),
        ],
        compiler_params=pltpu.CompilerParams(
            dimension_semantics=("arbitrary", "arbitrary"),
            vmem_limit_bytes=56 * MIB),
        name="moba_out",
    )(q, k4, vt4, kmean, yabd, x2, mod_l, out_g, w_out)


def kernel(x, c, ada_w, ada_b, norm_g, ffn1_w13, ffn1_w2, w_in, pool_w, pool_scale, sgu_w,
           sgu_b, sgu_norm_g, q_norm_g, k_norm_g, conv_w, out_norm_g, w_out, ffn2_w13, ffn2_w2):
    n_batch, seq, d = x.shape
    n_layers = ada_w.shape[0]
    assert seq % MOBA_BLOCK == 0 and seq % FFN_TM == 0 and MOBA_BLOCK % SGU_CHUNK == 0
    assert n_batch <= MOD_ROWS and w_in.shape[2] == N_GROUPS_IN * GROUP_W
    n_blk = seq // MOBA_BLOCK

    c_pad = jnp.zeros((MOD_ROWS, d), F32).at[:n_batch].set(c)
    mod = _ada_call(c_pad, ada_w, ada_b)[:, :n_batch].reshape(n_layers, n_batch, 9, d)

    ffn1_w13, ffn1_w2, ffn2_w13, ffn2_w2, w_in, w_out, pool_w = (
        w.astype(BF16) for w in (ffn1_w13, ffn1_w2, ffn2_w13, ffn2_w2, w_in, w_out, pool_w))

    x2 = x.reshape(n_batch * seq, d)
    for l in range(n_layers):
        mod_l = mod[l]
        x2 = _ffn_call(x2, mod_l, norm_g[l, 0:1], ffn1_w13, ffn1_w2, l, 0, seq)
        yabd, q, k, vt, kmean = _in_mix_call(
            x2, mod_l, norm_g[l, 1:2], w_in, pool_w,
            pool_scale[l].reshape(1, GROUP_W), sgu_w[l], sgu_b[l].T,
            sgu_norm_g[l].reshape(1, GROUP_W), q_norm_g[l].reshape(1, HEAD_DIM),
            k_norm_g[l].reshape(1, HEAD_DIM), conv_w[l],
            out_norm_g[l].reshape(4, GROUP_W), l, seq)
        x2 = _moba_out_call(
            q, k.reshape(n_batch, n_blk, MOBA_BLOCK, GROUP_W),
            vt.reshape(n_batch, n_blk, GROUP_W, MOBA_BLOCK),
            kmean.reshape(n_batch, n_blk, GROUP_W), yabd, x2, mod_l,
            out_norm_g[l].reshape(4, GROUP_W), w_out, l)
        x2 = _ffn_call(x2, mod_l, norm_g[l, 2:3], ffn2_w13, ffn2_w2, l, 2, seq)
    return x2.reshape(n_batch, seq, d)
```

```python
import functools

import jax
import jax.numpy as jnp
from jax import lax
from jax.experimental import pallas as pl
from jax.experimental.pallas import tpu as pltpu

F32 = jnp.float32
BF16 = jnp.bfloat16

GROUP_W = 512
HEAD_DIM = 128
N_HEADS = GROUP_W // HEAD_DIM
N_GROUPS_IN = 9
POOL_WINDOWS = (2, 4, 8, 16)
POOL_HALO = 16
SGU_CHUNK = 128
MOBA_BLOCK = 256
MOBA_TOPK = 3
CONV_WIDTH = 3
CONV_HALO = 8
FFN_RES = 0.5
EPS = 1e-6
MOD_ROWS = 8

ADA_TN = 1024
MIB = 1024 * 1024
FFN_TM = 1024
FFN_SUB_ROWS = 512
FFN_TF = 512
OUT_CHUNK = 512
MOBA_VARIANTS = 4
PROJ_CHUNK = 256
NORM_ROWS = 16
NORM_UNROLL = 8
FFN_VMEM_LIMIT = (2 * 2 * FFN_TM * 2048 * 4 + FFN_TM * 2048 * 2 + 2 * 3 * 2048 * FFN_TF * 2
                  + FFN_SUB_ROWS * (3 * FFN_TF + OUT_CHUNK) * 4 + 4 * MIB)


def _rms(x, g):
    ms = jnp.mean(x * x, axis=-1, keepdims=True)
    return x * lax.rsqrt(ms + EPS) * g


def _silu(x):
    return x * (1.0 / (1.0 + jnp.exp(-x)))


def _gelu_tanh(x):
    c = 0.7978845608028654
    return x * (0.5 * (1.0 + jnp.tanh(c * (x + 0.044715 * (x * x * x)))))


def _dot(a, b):
    return jnp.dot(a, b, preferred_element_type=F32)


def _dot_nt(a, b, precision=None):
    return lax.dot_general(a, b, (((1,), (1,)), ((), ())),
                           precision=precision, preferred_element_type=F32)


def _ada_kernel(c_ref, w_ref, b_ref, o_ref):
    s = _silu(c_ref[...]).astype(BF16)
    o_ref[0] = _dot(s, w_ref[0].astype(BF16)) + b_ref[0]


def _ada_call(c_pad, ada_w, ada_b):
    n_layers, d, n_out = ada_w.shape
    return pl.pallas_call(
        _ada_kernel,
        out_shape=jax.ShapeDtypeStruct((n_layers, MOD_ROWS, n_out), F32),
        grid=(n_layers, n_out // ADA_TN),
        in_specs=[
            pl.BlockSpec((MOD_ROWS, d), lambda l, n: (0, 0)),
            pl.BlockSpec((1, d, ADA_TN), lambda l, n: (l, 0, n)),
            pl.BlockSpec((1, 1, ADA_TN), lambda l, n: (l, 0, n)),
        ],
        out_specs=pl.BlockSpec((1, MOD_ROWS, ADA_TN), lambda l, n: (l, 0, n)),
        compiler_params=pltpu.CompilerParams(
            dimension_semantics=("arbitrary", "arbitrary"),
            vmem_limit_bytes=32 * MIB),
        name="ada_mod",
    )(c_pad, ada_w, ada_b.reshape(n_layers, 1, n_out))


def _ffn_kernel(x_ref, mod_ref, g_ref, w1_ref, w3_ref, w2_ref, o_ref, h_ref, *, sub):
    j = pl.program_id(1)
    tm, d_out = o_ref.shape

    @pl.when(j == 0)
    def _():
        shift = mod_ref[0, 3 * sub:3 * sub + 1, :]
        gain = g_ref[...] * (1.0 + mod_ref[0, 3 * sub + 1:3 * sub + 2, :])

        def norm_rows(r, carry):
            rows = pl.ds(pl.multiple_of(r * NORM_ROWS, NORM_ROWS), NORM_ROWS)
            x = x_ref[rows, :]
            inv = lax.rsqrt(jnp.mean(x * x, axis=-1, keepdims=True) + EPS)
            h_ref[rows, :] = (x * inv * gain + shift).astype(BF16)
            o_ref[rows, :] = jnp.zeros((NORM_ROWS, d_out), F32)
            return carry

        lax.fori_loop(0, tm // NORM_ROWS, norm_rows, 0, unroll=NORM_UNROLL)

    for r in range(tm // FFN_SUB_ROWS):
        rows = slice(r * FFN_SUB_ROWS, (r + 1) * FFN_SUB_ROWS)
        h = h_ref[rows, :]
        a = _dot(h, w1_ref[...])
        b = _dot(h, w3_ref[...])
        act = (_silu(a) * b).astype(BF16)
        for n in range(d_out // OUT_CHUNK):
            cs = slice(n * OUT_CHUNK, (n + 1) * OUT_CHUNK)
            o_ref[rows, cs] += _dot(act, w2_ref[:, cs])

    @pl.when(j == pl.num_programs(1) - 1)
    def _():
        gate = FFN_RES * mod_ref[0, 3 * sub + 2:3 * sub + 3, :]

        def residual_rows(r, carry):
            rows = pl.ds(pl.multiple_of(r * NORM_ROWS, NORM_ROWS), NORM_ROWS)
            o_ref[rows, :] = x_ref[rows, :] + gate * o_ref[rows, :]
            return carry

        lax.fori_loop(0, tm // NORM_ROWS, residual_rows, 0, unroll=NORM_UNROLL)


def _ffn_call(x2, mod_l, g, w13, w2, layer, sub, seq):
    m, d = x2.shape
    d_ff = w2.shape[1]
    tiles_per_batch = seq // FFN_TM
    n_ff = d_ff // FFN_TF
    return pl.pallas_call(
        functools.partial(_ffn_kernel, sub=sub),
        out_shape=jax.ShapeDtypeStruct((m, d), F32),
        grid=(m // FFN_TM, n_ff),
        in_specs=[
            pl.BlockSpec((FFN_TM, d), lambda i, j: (i, 0)),
            pl.BlockSpec((1, 9, d), lambda i, j: (i // tiles_per_batch, 0, 0)),
            pl.BlockSpec((1, d), lambda i, j: (0, 0)),
            pl.BlockSpec((None, d, FFN_TF), lambda i, j: (layer, 0, j)),
            pl.BlockSpec((None, d, FFN_TF), lambda i, j: (layer, 0, j + n_ff)),
            pl.BlockSpec((None, FFN_TF, d), lambda i, j: (layer, j, 0)),
        ],
        out_specs=pl.BlockSpec((FFN_TM, d), lambda i, j: (i, 0)),
        scratch_shapes=[pltpu.VMEM((FFN_TM, d), BF16)],
        compiler_params=pltpu.CompilerParams(
            dimension_semantics=("arbitrary", "arbitrary"),
            vmem_limit_bytes=FFN_VMEM_LIMIT),
        name=f"ffn{sub}",
    )(x2, mod_l, g, w13, w13, w2)


def _in_mix_kernel(x_ref, mod_ref, g_ref, win_ref, poolw_ref, pools_ref, sguw_ref,
                   sgubt_ref, sgug_ref, gq_ref, gk_ref, convw_ref, ong_ref,
                   yabd_ref, q_ref, k_ref, vt_ref, kmean_ref,
                   pbuf, zbuf, *, tiles_per_batch):
    tm = x_ref.shape[0]
    tib = pl.program_id(0) % tiles_per_batch

    @pl.when(tib == 0)
    def _():
        pbuf[0:POOL_HALO, :] = jnp.zeros((POOL_HALO, GROUP_W), F32)
        zbuf[0:CONV_HALO, :] = jnp.zeros((CONV_HALO, GROUP_W), F32)

    shift = mod_ref[0, 3:4, :]
    scale = mod_ref[0, 4:5, :]
    h = (_rms(x_ref[...], g_ref[...]) * (1.0 + scale) + shift).astype(BF16)

    proj = {g: _dot(h, win_ref[:, g * GROUP_W:(g + 1) * GROUP_W])
            for g in (0, 1, 2, 6, 7, 8, 3, 4, 5)}

    def heads():
        return [slice(hd * HEAD_DIM, (hd + 1) * HEAD_DIM) for hd in range(N_HEADS)]

    pbuf[POOL_HALO:POOL_HALO + tm, :] = proj[0]
    pos = tib * tm + lax.broadcasted_iota(jnp.int32, (tm, 1), 0)
    ya_parts = []
    for hd, (cs, win) in enumerate(zip(heads(), POOL_WINDOWS)):
        xg = pbuf[POOL_HALO:POOL_HALO + tm, cs]
        wsum = xg
        for lag in range(1, win):
            wsum = wsum + pbuf[POOL_HALO - lag:POOL_HALO - lag + tm, cs]
        cnt = jnp.minimum(pos + 1, win).astype(F32)
        dlt = wsum / cnt - xg
        ya_parts.append(_dot(dlt.astype(BF16), poolw_ref[hd]))
    ya = jnp.concatenate(ya_parts, axis=1) * pools_ref[...]
    pbuf[0:POOL_HALO, :] = pbuf[tm:tm + POOL_HALO, :]
    yabd_ref[:, 0:GROUP_W] = _rms(ya, ong_ref[0:1, :]).astype(BF16)

    u = _gelu_tanh(proj[1])
    v = _gelu_tanh(proj[2])
    r_t = lax.broadcasted_iota(jnp.int32, (SGU_CHUNK, SGU_CHUNK), 0)
    r_s = lax.broadcasted_iota(jnp.int32, (SGU_CHUNK, SGU_CHUNK), 1)
    yb_parts = []
    for hd, cs in enumerate(heads()):
        vh = _rms(v[:, cs], sgug_ref[:, cs]).astype(BF16)
        wm = jnp.where(r_t >= r_s, sguw_ref[hd], 0.0).astype(BF16)
        bias = sgubt_ref[:, hd:hd + 1]
        mixed = [_dot(wm, vh[c * SGU_CHUNK:(c + 1) * SGU_CHUNK, :]) + bias
                 for c in range(tm // SGU_CHUNK)]
        yb_parts.append(u[:, cs] * jnp.concatenate(mixed, axis=0))
    yb = jnp.concatenate(yb_parts, axis=1)
    yabd_ref[:, GROUP_W:2 * GROUP_W] = _rms(yb, ong_ref[1:2, :]).astype(BF16)

    gate_b = proj[6]
    zbuf[CONV_HALO:CONV_HALO + tm, :] = proj[7] * proj[8]
    conv = jnp.zeros((tm, GROUP_W), F32)
    for tap in range(CONV_WIDTH):
        off = CONV_HALO - (CONV_WIDTH - 1) + tap
        conv = conv + convw_ref[tap:tap + 1, :] * zbuf[off:off + tm, :]
    zbuf[0:CONV_HALO, :] = zbuf[tm:tm + CONV_HALO, :]
    yabd_ref[:, 2 * GROUP_W:3 * GROUP_W] = _rms(gate_b * conv, ong_ref[3:4, :]).astype(BF16)

    q_raw = proj[3]
    k_raw = proj[4]
    qk_scale = HEAD_DIM ** -0.5
    q = jnp.concatenate([_rms(q_raw[:, cs], gq_ref[...]) * qk_scale for cs in heads()], axis=1)
    k = jnp.concatenate([_rms(k_raw[:, cs], gk_ref[...]) for cs in heads()], axis=1)
    q_ref[...] = q
    k_ref[...] = k.astype(BF16)
    kmean_ref[0] = jnp.mean(k, axis=0, keepdims=True)
    vt_ref[0] = proj[5].T.astype(BF16)


def _in_mix_call(x2, mod_l, g, w_in, pool_w, pool_scale, sgu_w, sgu_bt, sgu_g,
                 gq, gk, conv_w, out_g, layer, seq):
    m, d = x2.shape
    tm = MOBA_BLOCK
    n_tiles = m // tm
    tiles_per_batch = seq // tm
    const2 = lambda i: (0, 0)
    const3 = lambda i: (0, 0, 0)
    return pl.pallas_call(
        functools.partial(_in_mix_kernel, tiles_per_batch=tiles_per_batch),
        out_shape=(
            jax.ShapeDtypeStruct((m, 3 * GROUP_W), BF16),
            jax.ShapeDtypeStruct((m, GROUP_W), F32),
            jax.ShapeDtypeStruct((m, GROUP_W), BF16),
            jax.ShapeDtypeStruct((n_tiles, GROUP_W, tm), BF16),
            jax.ShapeDtypeStruct((n_tiles, 1, GROUP_W), F32),
        ),
        grid=(n_tiles,),
        in_specs=[
            pl.BlockSpec((tm, d), lambda i: (i, 0)),
            pl.BlockSpec((1, 9, d), lambda i: (i // tiles_per_batch, 0, 0)),
            pl.BlockSpec((1, d), const2),
            pl.BlockSpec((None,) + w_in.shape[1:], lambda i: (layer, 0, 0)),
            pl.BlockSpec((None,) + pool_w.shape[1:], lambda i: (layer, 0, 0, 0)),
            pl.BlockSpec((1, GROUP_W), const2),
            pl.BlockSpec(sgu_w.shape, const3),
            pl.BlockSpec(sgu_bt.shape, const2),
            pl.BlockSpec((1, GROUP_W), const2),
            pl.BlockSpec((1, HEAD_DIM), const2),
            pl.BlockSpec((1, HEAD_DIM), const2),
            pl.BlockSpec(conv_w.shape, const2),
            pl.BlockSpec(out_g.shape, const2),
        ],
        out_specs=(
            pl.BlockSpec((tm, 3 * GROUP_W), lambda i: (i, 0)),
            pl.BlockSpec((tm, GROUP_W), lambda i: (i, 0)),
            pl.BlockSpec((tm, GROUP_W), lambda i: (i, 0)),
            pl.BlockSpec((1, GROUP_W, tm), lambda i: (i, 0, 0)),
            pl.BlockSpec((1, 1, GROUP_W), lambda i: (i, 0, 0)),
        ),
        scratch_shapes=[
            pltpu.VMEM((POOL_HALO + tm, GROUP_W), F32),
            pltpu.VMEM((CONV_HALO + tm, GROUP_W), F32),
        ],
        compiler_params=pltpu.CompilerParams(
            dimension_semantics=("arbitrary",),
            vmem_limit_bytes=56 * MIB),
        name="in_mix",
    )(x2, mod_l, g, w_in, pool_w, pool_scale, sgu_w, sgu_bt, sgu_g, gq, gk, conv_w, out_g)


def _moba_out_kernel(q_ref, k_ref, vt_ref, kmean_ref, yabd_ref, x_ref, mod_ref, ong_ref,
                     wout_ref, o_ref, qb_ref):
    own = pl.program_id(1)
    tq = q_ref.shape[0]
    n_blk = kmean_ref.shape[1]
    neg_inf = -jnp.inf
    heads = [slice(hd * HEAD_DIM, (hd + 1) * HEAD_DIM) for hd in range(N_HEADS)]

    def tile(n_visit):
        blk_id = lax.broadcasted_iota(jnp.int32, (n_blk, tq), 0).astype(F32)
        past = blk_id < own.astype(F32)
        key_i = lax.broadcasted_iota(jnp.int32, (tq, tq), 0)
        qry_i = lax.broadcasted_iota(jnp.int32, (tq, tq), 1)

        d_out = o_ref.shape[1]
        n_chunks = d_out // PROJ_CHUNK
        proj_local = []

        def issue_proj_chunks(units_done):
            while len(proj_local) * (n_visit + 1) < units_done * n_chunks:
                cs = slice(len(proj_local) * PROJ_CHUNK, (len(proj_local) + 1) * PROJ_CHUNK)
                proj_local.append(
                    _dot(yabd_ref[:, 0:2 * GROUP_W], wout_ref[0:2 * GROUP_W, cs])
                    + _dot(yabd_ref[:, 2 * GROUP_W:3 * GROUP_W],
                           wout_ref[3 * GROUP_W:4 * GROUP_W, cs]))

        qb_ref[...] = q_ref[...].astype(BF16)
        blk_scores = [_dot_nt(kmean_ref[0, :, cs], q_ref[:, cs], precision=lax.Precision.HIGHEST)
                      for cs in heads]
        own_scores = [_dot_nt(k_ref[0, own, :, cs], qb_ref[:, cs]) for cs in heads]

        def block_scores(nb):
            return [_dot_nt(k_ref[0, nb, :, cs], qb_ref[:, cs]) for cs in heads]

        scores_next = block_scores(0) if n_visit else None
        issue_proj_chunks(1)
        sel, m, l, acc = [], [], [], []
        for hd, cs in enumerate(heads):
            sb = jnp.where(past, blk_scores[hd], neg_inf)
            chosen = jnp.zeros((n_blk, tq), F32)
            for _ in range(min(MOBA_TOPK, n_blk)):
                top = jnp.max(sb, axis=0, keepdims=True)
                idx = jnp.min(jnp.where(sb == top, blk_id, float(n_blk)), axis=0, keepdims=True)
                hit = blk_id == idx
                chosen = jnp.where(hit, 1.0, chosen)
                sb = jnp.where(hit, neg_inf, sb)
            sel.append(jnp.where(past, chosen, 0.0))

            s = jnp.where(key_i <= qry_i, own_scores[hd], neg_inf)
            m.append(jnp.max(s, axis=0, keepdims=True))
            p = jnp.exp(s - m[hd])
            l.append(jnp.sum(p, axis=0, keepdims=True))
            acc.append(_dot(vt_ref[0, own, cs, :], p.astype(BF16)))

        for nb in range(n_visit):
            scores = scores_next
            if nb + 1 < n_visit:
                scores_next = block_scores(nb + 1)
            issue_proj_chunks(nb + 2)
            for hd, cs in enumerate(heads):
                s = jnp.where(sel[hd][nb:nb + 1, :] > 0.0, scores[hd], neg_inf)
                m_new = jnp.maximum(m[hd], jnp.max(s, axis=0, keepdims=True))
                alpha = jnp.exp(m[hd] - m_new)
                p = jnp.exp(s - m_new)
                l[hd] = alpha * l[hd] + jnp.sum(p, axis=0, keepdims=True)
                acc[hd] = alpha * acc[hd] + _dot(vt_ref[0, nb, cs, :], p.astype(BF16))
                m[hd] = m_new

        yc = jnp.concatenate([(acc[hd] / l[hd]).T for hd in range(N_HEADS)], axis=1)
        proj = jnp.concatenate(proj_local, axis=1) + _dot(
            _rms(yc, ong_ref[2:3, :]).astype(BF16), wout_ref[2 * GROUP_W:3 * GROUP_W, :])
        o_ref[...] = x_ref[...] + mod_ref[0, 5:6, :] * proj

    step = n_blk // MOBA_VARIANTS
    for v in range(MOBA_VARIANTS):
        in_range = jnp.logical_and(own >= v * step, own < (v + 1) * step)
        pl.when(in_range)(functools.partial(tile, (v + 1) * step - 1))


def _moba_out_call(q, k4, vt4, kmean, yabd, x2, mod_l, out_g, w_out, layer):
    m, d = x2.shape
    n_batch, n_blk, tq, _ = k4.shape
    row = lambda b, j: (b * n_blk + j, 0)
    return pl.pallas_call(
        _moba_out_kernel,
        out_shape=jax.ShapeDtypeStruct((m, d), F32),
        grid=(n_batch, n_blk),
        in_specs=[
            pl.BlockSpec((tq, GROUP_W), row),
            pl.BlockSpec((1, n_blk, tq, GROUP_W), lambda b, j: (b, 0, 0, 0)),
            pl.BlockSpec((1, n_blk, GROUP_W, tq), lambda b, j: (b, 0, 0, 0)),
            pl.BlockSpec((1, n_blk, GROUP_W), lambda b, j: (b, 0, 0)),
            pl.BlockSpec((tq, 3 * GROUP_W), row),
            pl.BlockSpec((tq, d), row),
            pl.BlockSpec((1, 9, d), lambda b, j: (b, 0, 0)),
            pl.BlockSpec(out_g.shape, lambda b, j: (0, 0)),
            pl.BlockSpec((None,) + w_out.shape[1:], lambda b, j: (layer, 0, 0)),
        ],
        out_specs=pl.BlockSpec((tq, d), row),
        scratch_shapes=[pltpu.VMEM((tq, GROUP_W), BF16)],
        compiler_params=pltpu.CompilerParams(
            dimension_semantics=("arbitrary", "arbitrary"),
            vmem_limit_bytes=56 * MIB),
        name="moba_out",
    )(q, k4, vt4, kmean, yabd, x2, mod_l, out_g, w_out)


def kernel(x, c, ada_w, ada_b, norm_g, ffn1_w13, ffn1_w2, w_in, pool_w, pool_scale, sgu_w,
           sgu_b, sgu_norm_g, q_norm_g, k_norm_g, conv_w, out_norm_g, w_out, ffn2_w13, ffn2_w2):
    n_batch, seq, d = x.shape
    n_layers = ada_w.shape[0]
    assert seq % MOBA_BLOCK == 0 and seq % FFN_TM == 0 and MOBA_BLOCK % SGU_CHUNK == 0
    assert n_batch <= MOD_ROWS and w_in.shape[2] == N_GROUPS_IN * GROUP_W
    n_blk = seq // MOBA_BLOCK

    c_pad = jnp.zeros((MOD_ROWS, d), F32).at[:n_batch].set(c)
    mod = _ada_call(c_pad, ada_w, ada_b)[:, :n_batch].reshape(n_layers, n_batch, 9, d)

    ffn1_w13, ffn1_w2, ffn2_w13, ffn2_w2, w_in, w_out, pool_w = (
        w.astype(BF16) for w in (ffn1_w13, ffn1_w2, ffn2_w13, ffn2_w2, w_in, w_out, pool_w))

    x2 = x.reshape(n_batch * seq, d)
    for l in range(n_layers):
        mod_l = mod[l]
        x2 = _ffn_call(x2, mod_l, norm_g[l, 0:1], ffn1_w13, ffn1_w2, l, 0, seq)
        yabd, q, k, vt, kmean = _in_mix_call(
            x2, mod_l, norm_g[l, 1:2], w_in, pool_w,
            pool_scale[l].reshape(1, GROUP_W), sgu_w[l], sgu_b[l].T,
            sgu_norm_g[l].reshape(1, GROUP_W), q_norm_g[l].reshape(1, HEAD_DIM),
            k_norm_g[l].reshape(1, HEAD_DIM), conv_w[l],
            out_norm_g[l].reshape(4, GROUP_W), l, seq)
        x2 = _moba_out_call(
            q, k.reshape(n_batch, n_blk, MOBA_BLOCK, GROUP_W),
            vt.reshape(n_batch, n_blk, GROUP_W, MOBA_BLOCK),
            kmean.reshape(n_batch, n_blk, GROUP_W), yabd, x2, mod_l,
            out_norm_g[l].reshape(4, GROUP_W), w_out, l)
        x2 = _ffn_call(x2, mod_l, norm_g[l, 2:3], ffn2_w13, ffn2_w2, l, 2, seq)
    return x2.reshape(n_batch, seq, d)
```

```python
import functools

import jax
import jax.numpy as jnp
from jax import lax
from jax.experimental import pallas as pl
from jax.experimental.pallas import tpu as pltpu

F32 = jnp.float32
BF16 = jnp.bfloat16

GROUP_W = 512
HEAD_DIM = 128
N_HEADS = GROUP_W // HEAD_DIM
N_GROUPS_IN = 9
POOL_WINDOWS = (2, 4, 8, 16)
POOL_HALO = 16
SGU_CHUNK = 128
MOBA_BLOCK = 256
MOBA_TOPK = 3
CONV_WIDTH = 3
CONV_HALO = 8
FFN_RES = 0.5
EPS = 1e-6
MOD_ROWS = 8
BF16_ROWS = 16

ADA_TN = 1024
MIB = 1024 * 1024
FFN_TM = 1024
FFN_SUB_ROWS = 512
FFN_TF = 512
OUT_CHUNK = 512
MOBA_VARIANTS = 4
PROJ_CHUNK = 256
NORM_ROWS = 16
NORM_UNROLL = 8
FFN_VMEM_LIMIT = (2 * 2 * FFN_TM * 2048 * 4 + FFN_TM * 2048 * 2 + 2 * 3 * 2048 * FFN_TF * 2
                  + FFN_SUB_ROWS * (3 * FFN_TF + OUT_CHUNK) * 4 + 4 * MIB)


def _rms(x, g):
    ms = jnp.mean(x * x, axis=-1, keepdims=True)
    return x * lax.rsqrt(ms + EPS) * g


def _silu(x):
    return x * (1.0 / (1.0 + jnp.exp(-x)))


def _gelu_tanh(x):
    c = 0.7978845608028654
    return x * (0.5 * (1.0 + jnp.tanh(c * (x + 0.044715 * (x * x * x)))))


def _dot(a, b):
    return jnp.dot(a, b, preferred_element_type=F32)


def _dot_nt(a, b, precision=None):
    return lax.dot_general(a, b, (((1,), (1,)), ((), ())),
                           precision=precision, preferred_element_type=F32)


def _cast_specs(casts, n_steps, step_of):
    in_specs, out_specs, out_shapes = [], [], []
    for w, layer in casts:
        _, rows, cols = w.shape
        n_blocks = max(nb for nb in range(1, n_steps + 1)
                       if rows % nb == 0 and (rows // nb) % BF16_ROWS == 0)

        def block_of(*g, n_blocks=n_blocks):
            return jnp.minimum(step_of(*g), n_blocks - 1)

        in_specs.append(pl.BlockSpec(
            (None, rows // n_blocks, cols),
            lambda *g, block_of=block_of, layer=layer: (layer, block_of(*g), 0)))
        out_specs.append(pl.BlockSpec(
            (rows // n_blocks, cols), lambda *g, block_of=block_of: (block_of(*g), 0)))
        out_shapes.append(jax.ShapeDtypeStruct((rows, cols), BF16))
    return in_specs, out_specs, out_shapes


def _cast_bytes(casts, n_steps):
    specs, _, _ = _cast_specs(casts, n_steps, lambda *g: 0)
    return sum(2 * (4 + 2) * s.block_shape[1] * s.block_shape[2] for s in specs)


def _run_casts(src_refs, dst_refs):
    for src, dst in zip(src_refs, dst_refs):
        dst[...] = src[...].astype(BF16)


def _ada_kernel(c_ref, w_ref, b_ref, *refs, n_cast):
    cast_src, o_ref, cast_dst = refs[:n_cast], refs[n_cast], refs[n_cast + 1:]
    s = _silu(c_ref[...]).astype(BF16)
    o_ref[0] = _dot(s, w_ref[0].astype(BF16)) + b_ref[0]
    _run_casts(cast_src, cast_dst)


def _ada_call(c_pad, ada_w, ada_b, casts):
    n_layers, d, n_out = ada_w.shape
    n_cols = n_out // ADA_TN
    cast_in, cast_out, cast_shapes = _cast_specs(casts, n_layers * n_cols,
                                                 lambda l, n: l * n_cols + n)
    return pl.pallas_call(
        functools.partial(_ada_kernel, n_cast=len(casts)),
        out_shape=[jax.ShapeDtypeStruct((n_layers, MOD_ROWS, n_out), F32)] + cast_shapes,
        grid=(n_layers, n_cols),
        in_specs=[
            pl.BlockSpec((MOD_ROWS, d), lambda l, n: (0, 0)),
            pl.BlockSpec((1, d, ADA_TN), lambda l, n: (l, 0, n)),
            pl.BlockSpec((1, 1, ADA_TN), lambda l, n: (l, 0, n)),
        ] + cast_in,
        out_specs=[pl.BlockSpec((1, MOD_ROWS, ADA_TN), lambda l, n: (l, 0, n))] + cast_out,
        compiler_params=pltpu.CompilerParams(
            dimension_semantics=("arbitrary", "arbitrary"),
            vmem_limit_bytes=(2 * d * ADA_TN * (4 + 2) + _cast_bytes(casts, n_layers * n_cols)
                              + 4 * MIB)),
        name="ada_mod",
    )(c_pad, ada_w, ada_b.reshape(n_layers, 1, n_out), *[w for w, _ in casts])


def _ffn_kernel(x_ref, mod_ref, g_ref, w1_ref, w3_ref, w2_ref, o_ref, h_ref, *, sub):
    j = pl.program_id(1)
    tm, d_out = o_ref.shape

    @pl.when(j == 0)
    def _():
        shift = mod_ref[0, 3 * sub:3 * sub + 1, :]
        gain = g_ref[...] * (1.0 + mod_ref[0, 3 * sub + 1:3 * sub + 2, :])

        def norm_rows(r, carry):
            rows = pl.ds(pl.multiple_of(r * NORM_ROWS, NORM_ROWS), NORM_ROWS)
            x = x_ref[rows, :]
            inv = lax.rsqrt(jnp.mean(x * x, axis=-1, keepdims=True) + EPS)
            h_ref[rows, :] = (x * inv * gain + shift).astype(BF16)
            o_ref[rows, :] = jnp.zeros((NORM_ROWS, d_out), F32)
            return carry

        lax.fori_loop(0, tm // NORM_ROWS, norm_rows, 0, unroll=NORM_UNROLL)

    for r in range(tm // FFN_SUB_ROWS):
        rows = slice(r * FFN_SUB_ROWS, (r + 1) * FFN_SUB_ROWS)
        h = h_ref[rows, :]
        a = _dot(h, w1_ref[...])
        b = _dot(h, w3_ref[...])
        act = (_silu(a) * b).astype(BF16)
        for n in range(d_out // OUT_CHUNK):
            cs = slice(n * OUT_CHUNK, (n + 1) * OUT_CHUNK)
            o_ref[rows, cs] += _dot(act, w2_ref[:, cs])

    @pl.when(j == pl.num_programs(1) - 1)
    def _():
        gate = FFN_RES * mod_ref[0, 3 * sub + 2:3 * sub + 3, :]

        def residual_rows(r, carry):
            rows = pl.ds(pl.multiple_of(r * NORM_ROWS, NORM_ROWS), NORM_ROWS)
            o_ref[rows, :] = x_ref[rows, :] + gate * o_ref[rows, :]
            return carry

        lax.fori_loop(0, tm // NORM_ROWS, residual_rows, 0, unroll=NORM_UNROLL)


def _ffn_call(x2, mod_l, g, w13, w2, sub, seq):
    m, d = x2.shape
    d_ff = w2.shape[0]
    tiles_per_batch = seq // FFN_TM
    n_ff = d_ff // FFN_TF
    return pl.pallas_call(
        functools.partial(_ffn_kernel, sub=sub),
        out_shape=jax.ShapeDtypeStruct((m, d), F32),
        grid=(m // FFN_TM, n_ff),
        in_specs=[
            pl.BlockSpec((FFN_TM, d), lambda i, j: (i, 0)),
            pl.BlockSpec((1, 9, d), lambda i, j: (i // tiles_per_batch, 0, 0)),
            pl.BlockSpec((1, d), lambda i, j: (0, 0)),
            pl.BlockSpec((d, FFN_TF), lambda i, j: (0, j)),
            pl.BlockSpec((d, FFN_TF), lambda i, j: (0, j + n_ff)),
            pl.BlockSpec((FFN_TF, d), lambda i, j: (j, 0)),
        ],
        out_specs=pl.BlockSpec((FFN_TM, d), lambda i, j: (i, 0)),
        scratch_shapes=[pltpu.VMEM((FFN_TM, d), BF16)],
        compiler_params=pltpu.CompilerParams(
            dimension_semantics=("arbitrary", "arbitrary"),
            vmem_limit_bytes=FFN_VMEM_LIMIT),
        name=f"ffn{sub}",
    )(x2, mod_l, g, w13, w13, w2)


def _in_mix_kernel(x_ref, mod_ref, g_ref, win_ref, poolw_ref, pools_ref, sguw_ref,
                   sgubt_ref, sgug_ref, gq_ref, gk_ref, convw_ref, ong_ref, *refs,
                   tiles_per_batch, n_cast):
    cast_src, refs = refs[:n_cast], refs[n_cast:]
    yabd_ref, q_ref, k_ref, vt_ref, kmean_ref = refs[:5]
    cast_dst, (pbuf, zbuf) = refs[5:5 + n_cast], refs[5 + n_cast:]
    _run_casts(cast_src, cast_dst)
    tm = x_ref.shape[0]
    tib = pl.program_id(0) % tiles_per_batch

    @pl.when(tib == 0)
    def _():
        pbuf[0:POOL_HALO, :] = jnp.zeros((POOL_HALO, GROUP_W), F32)
        zbuf[0:CONV_HALO, :] = jnp.zeros((CONV_HALO, GROUP_W), F32)

    shift = mod_ref[0, 3:4, :]
    scale = mod_ref[0, 4:5, :]
    h = (_rms(x_ref[...], g_ref[...]) * (1.0 + scale) + shift).astype(BF16)

    proj = {g: _dot(h, win_ref[:, g * GROUP_W:(g + 1) * GROUP_W])
            for g in (0, 1, 2, 6, 7, 8, 3, 4, 5)}

    def heads():
        return [slice(hd * HEAD_DIM, (hd + 1) * HEAD_DIM) for hd in range(N_HEADS)]

    pbuf[POOL_HALO:POOL_HALO + tm, :] = proj[0]
    pos = tib * tm + lax.broadcasted_iota(jnp.int32, (tm, 1), 0)
    ya_parts = []
    for hd, (cs, win) in enumerate(zip(heads(), POOL_WINDOWS)):
        xg = pbuf[POOL_HALO:POOL_HALO + tm, cs]
        wsum = xg
        for lag in range(1, win):
            wsum = wsum + pbuf[POOL_HALO - lag:POOL_HALO - lag + tm, cs]
        cnt = jnp.minimum(pos + 1, win).astype(F32)
        dlt = wsum / cnt - xg
        ya_parts.append(_dot(dlt.astype(BF16), poolw_ref[hd].astype(BF16)))
    ya = jnp.concatenate(ya_parts, axis=1) * pools_ref[...]
    pbuf[0:POOL_HALO, :] = pbuf[tm:tm + POOL_HALO, :]
    yabd_ref[:, 0:GROUP_W] = _rms(ya, ong_ref[0:1, :]).astype(BF16)

    u = _gelu_tanh(proj[1])
    v = _gelu_tanh(proj[2])
    r_t = lax.broadcasted_iota(jnp.int32, (SGU_CHUNK, SGU_CHUNK), 0)
    r_s = lax.broadcasted_iota(jnp.int32, (SGU_CHUNK, SGU_CHUNK), 1)
    yb_parts = []
    for hd, cs in enumerate(heads()):
        vh = _rms(v[:, cs], sgug_ref[:, cs]).astype(BF16)
        wm = jnp.where(r_t >= r_s, sguw_ref[hd], 0.0).astype(BF16)
        bias = sgubt_ref[:, hd:hd + 1]
        mixed = [_dot(wm, vh[c * SGU_CHUNK:(c + 1) * SGU_CHUNK, :]) + bias
                 for c in range(tm // SGU_CHUNK)]
        yb_parts.append(u[:, cs] * jnp.concatenate(mixed, axis=0))
    yb = jnp.concatenate(yb_parts, axis=1)
    yabd_ref[:, GROUP_W:2 * GROUP_W] = _rms(yb, ong_ref[1:2, :]).astype(BF16)

    gate_b = proj[6]
    zbuf[CONV_HALO:CONV_HALO + tm, :] = proj[7] * proj[8]
    conv = jnp.zeros((tm, GROUP_W), F32)
    for tap in range(CONV_WIDTH):
        off = CONV_HALO - (CONV_WIDTH - 1) + tap
        conv = conv + convw_ref[tap:tap + 1, :] * zbuf[off:off + tm, :]
    zbuf[0:CONV_HALO, :] = zbuf[tm:tm + CONV_HALO, :]
    yabd_ref[:, 2 * GROUP_W:3 * GROUP_W] = _rms(gate_b * conv, ong_ref[3:4, :]).astype(BF16)

    q_raw = proj[3]
    k_raw = proj[4]
    qk_scale = HEAD_DIM ** -0.5
    q = jnp.concatenate([_rms(q_raw[:, cs], gq_ref[...]) * qk_scale for cs in heads()], axis=1)
    k = jnp.concatenate([_rms(k_raw[:, cs], gk_ref[...]) for cs in heads()], axis=1)
    q_ref[...] = q
    k_ref[...] = k.astype(BF16)
    kmean_ref[0] = jnp.mean(k, axis=0, keepdims=True)
    vt_ref[0] = proj[5].T.astype(BF16)


def _in_mix_call(x2, mod_l, g, w_in, pool_w, pool_scale, sgu_w, sgu_bt, sgu_g,
                 gq, gk, conv_w, out_g, casts, seq):
    m, d = x2.shape
    tm = MOBA_BLOCK
    n_tiles = m // tm
    tiles_per_batch = seq // tm
    const2 = lambda i: (0, 0)
    const3 = lambda i: (0, 0, 0)
    cast_in, cast_out, cast_shapes = _cast_specs(casts, n_tiles, lambda i: i)
    vmem = (w_in.size * 2 + 2 * tm * d * 4 + 2 * tm * 6 * GROUP_W * 4
            + 2 * N_GROUPS_IN * tm * GROUP_W * 4 + _cast_bytes(casts, n_tiles) + 4 * MIB)
    outs = pl.pallas_call(
        functools.partial(_in_mix_kernel, tiles_per_batch=tiles_per_batch, n_cast=len(casts)),
        out_shape=[
            jax.ShapeDtypeStruct((m, 3 * GROUP_W), BF16),
            jax.ShapeDtypeStruct((m, GROUP_W), F32),
            jax.ShapeDtypeStruct((m, GROUP_W), BF16),
            jax.ShapeDtypeStruct((n_tiles, GROUP_W, tm), BF16),
            jax.ShapeDtypeStruct((n_tiles, 1, GROUP_W), F32),
        ] + cast_shapes,
        grid=(n_tiles,),
        in_specs=[
            pl.BlockSpec((tm, d), lambda i: (i, 0)),
            pl.BlockSpec((1, 9, d), lambda i: (i // tiles_per_batch, 0, 0)),
            pl.BlockSpec((1, d), const2),
            pl.BlockSpec(w_in.shape, const2),
            pl.BlockSpec(pool_w.shape, const3),
            pl.BlockSpec((1, GROUP_W), const2),
            pl.BlockSpec(sgu_w.shape, const3),
            pl.BlockSpec(sgu_bt.shape, const2),
            pl.BlockSpec((1, GROUP_W), const2),
            pl.BlockSpec((1, HEAD_DIM), const2),
            pl.BlockSpec((1, HEAD_DIM), const2),
            pl.BlockSpec(conv_w.shape, const2),
            pl.BlockSpec(out_g.shape, const2),
        ] + cast_in,
        out_specs=[
            pl.BlockSpec((tm, 3 * GROUP_W), lambda i: (i, 0)),
            pl.BlockSpec((tm, GROUP_W), lambda i: (i, 0)),
            pl.BlockSpec((tm, GROUP_W), lambda i: (i, 0)),
            pl.BlockSpec((1, GROUP_W, tm), lambda i: (i, 0, 0)),
            pl.BlockSpec((1, 1, GROUP_W), lambda i: (i, 0, 0)),
        ] + cast_out,
        scratch_shapes=[
            pltpu.VMEM((POOL_HALO + tm, GROUP_W), F32),
            pltpu.VMEM((CONV_HALO + tm, GROUP_W), F32),
        ],
        compiler_params=pltpu.CompilerParams(
            dimension_semantics=("arbitrary",),
            vmem_limit_bytes=vmem),
        name="in_mix",
    )(x2, mod_l, g, w_in, pool_w, pool_scale, sgu_w, sgu_bt, sgu_g, gq, gk, conv_w, out_g,
      *[w for w, _ in casts])
    return outs[:5], outs[5:]


def _moba_out_kernel(q_ref, k_ref, vt_ref, kmean_ref, yabd_ref, x_ref, mod_ref, ong_ref,
                     wout_ref, *refs, n_cast):
    cast_src, o_ref = refs[:n_cast], refs[n_cast]
    cast_dst, qb_ref = refs[n_cast + 1:2 * n_cast + 1], refs[2 * n_cast + 1]
    _run_casts(cast_src, cast_dst)
    own = pl.program_id(1)
    tq = q_ref.shape[0]
    n_blk = kmean_ref.shape[1]
    neg_inf = -jnp.inf
    heads = [slice(hd * HEAD_DIM, (hd + 1) * HEAD_DIM) for hd in range(N_HEADS)]

    def tile(n_visit):
        blk_id = lax.broadcasted_iota(jnp.int32, (n_blk, tq), 0).astype(F32)
        past = blk_id < own.astype(F32)
        key_i = lax.broadcasted_iota(jnp.int32, (tq, tq), 0)
        qry_i = lax.broadcasted_iota(jnp.int32, (tq, tq), 1)

        d_out = o_ref.shape[1]
        n_chunks = d_out // PROJ_CHUNK
        proj_local = []

        def issue_proj_chunks(units_done):
            while len(proj_local) * (n_visit + 1) < units_done * n_chunks:
                cs = slice(len(proj_local) * PROJ_CHUNK, (len(proj_local) + 1) * PROJ_CHUNK)
                proj_local.append(
                    _dot(yabd_ref[:, 0:2 * GROUP_W], wout_ref[0:2 * GROUP_W, cs])
                    + _dot(yabd_ref[:, 2 * GROUP_W:3 * GROUP_W],
                           wout_ref[3 * GROUP_W:4 * GROUP_W, cs]))

        qb_ref[...] = q_ref[...].astype(BF16)
        blk_scores = [_dot_nt(kmean_ref[0, :, cs], q_ref[:, cs], precision=lax.Precision.HIGHEST)
                      for cs in heads]
        own_scores = [_dot_nt(k_ref[0, own, :, cs], qb_ref[:, cs]) for cs in heads]

        def block_scores(nb):
            return [_dot_nt(k_ref[0, nb, :, cs], qb_ref[:, cs]) for cs in heads]

        scores_next = block_scores(0) if n_visit else None
        issue_proj_chunks(1)
        sel, m, l, acc = [], [], [], []
        for hd, cs in enumerate(heads):
            sb = jnp.where(past, blk_scores[hd], neg_inf)
            chosen = jnp.zeros((n_blk, tq), F32)
            for _ in range(min(MOBA_TOPK, n_blk)):
                top = jnp.max(sb, axis=0, keepdims=True)
                idx = jnp.min(jnp.where(sb == top, blk_id, float(n_blk)), axis=0, keepdims=True)
                hit = blk_id == idx
                chosen = jnp.where(hit, 1.0, chosen)
                sb = jnp.where(hit, neg_inf, sb)
            sel.append(jnp.where(past, chosen, 0.0))

            s = jnp.where(key_i <= qry_i, own_scores[hd], neg_inf)
            m.append(jnp.max(s, axis=0, keepdims=True))
            p = jnp.exp(s - m[hd])
            l.append(jnp.sum(p, axis=0, keepdims=True))
            acc.append(_dot(vt_ref[0, own, cs, :], p.astype(BF16)))

        for nb in range(n_visit):
            scores = scores_next
            if nb + 1 < n_visit:
                scores_next = block_scores(nb + 1)
            issue_proj_chunks(nb + 2)
            for hd, cs in enumerate(heads):
                s = jnp.where(sel[hd][nb:nb + 1, :] > 0.0, scores[hd], neg_inf)
                m_new = jnp.maximum(m[hd], jnp.max(s, axis=0, keepdims=True))
                alpha = jnp.exp(m[hd] - m_new)
                p = jnp.exp(s - m_new)
                l[hd] = alpha * l[hd] + jnp.sum(p, axis=0, keepdims=True)
                acc[hd] = alpha * acc[hd] + _dot(vt_ref[0, nb, cs, :], p.astype(BF16))
                m[hd] = m_new

        yc = jnp.concatenate([(acc[hd] / l[hd]).T for hd in range(N_HEADS)], axis=1)
        proj = jnp.concatenate(proj_local, axis=1) + _dot(
            _rms(yc, ong_ref[2:3, :]).astype(BF16), wout_ref[2 * GROUP_W:3 * GROUP_W, :])
        o_ref[...] = x_ref[...] + mod_ref[0, 5:6, :] * proj

    step = n_blk // MOBA_VARIANTS
    for v in range(MOBA_VARIANTS):
        in_range = jnp.logical_and(own >= v * step, own < (v + 1) * step)
        pl.when(in_range)(functools.partial(tile, (v + 1) * step - 1))


def _moba_out_call(q, k4, vt4, kmean, yabd, x2, mod_l, out_g, w_out, casts):
    m, d = x2.shape
    n_batch, n_blk, tq, _ = k4.shape
    row = lambda b, j: (b * n_blk + j, 0)
    n_steps = n_batch * n_blk
    cast_in, cast_out, cast_shapes = _cast_specs(casts, n_steps, lambda b, j: b * n_blk + j)
    vmem = (2 * 2 * n_blk * tq * GROUP_W * 2 + w_out.size * 2 + 2 * 2 * tq * d * 4
            + 2 * tq * GROUP_W * (4 + 3 * 2) + 2 * tq * d * 4 + 6 * N_HEADS * tq * tq * 4
            + _cast_bytes(casts, n_steps) + 2 * MIB)
    outs = pl.pallas_call(
        functools.partial(_moba_out_kernel, n_cast=len(casts)),
        out_shape=[jax.ShapeDtypeStruct((m, d), F32)] + cast_shapes,
        grid=(n_batch, n_blk),
        in_specs=[
            pl.BlockSpec((tq, GROUP_W), row),
            pl.BlockSpec((1, n_blk, tq, GROUP_W), lambda b, j: (b, 0, 0, 0)),
            pl.BlockSpec((1, n_blk, GROUP_W, tq), lambda b, j: (b, 0, 0, 0)),
            pl.BlockSpec((1, n_blk, GROUP_W), lambda b, j: (b, 0, 0)),
            pl.BlockSpec((tq, 3 * GROUP_W), row),
            pl.BlockSpec((tq, d), row),
            pl.BlockSpec((1, 9, d), lambda b, j: (b, 0, 0)),
            pl.BlockSpec(out_g.shape, lambda b, j: (0, 0)),
            pl.BlockSpec(w_out.shape, lambda b, j: (0, 0)),
        ] + cast_in,
        out_specs=[pl.BlockSpec((tq, d), row)] + cast_out,
        scratch_shapes=[pltpu.VMEM((tq, GROUP_W), BF16)],
        compiler_params=pltpu.CompilerParams(
            dimension_semantics=("arbitrary", "arbitrary"),
            vmem_limit_bytes=vmem),
        name="moba_out",
    )(q, k4, vt4, kmean, yabd, x2, mod_l, out_g, w_out, *[w for w, _ in casts])
    return outs[0], outs[1:]


def kernel(x, c, ada_w, ada_b, norm_g, ffn1_w13, ffn1_w2, w_in, pool_w, pool_scale, sgu_w,
           sgu_b, sgu_norm_g, q_norm_g, k_norm_g, conv_w, out_norm_g, w_out, ffn2_w13, ffn2_w2):
    n_batch, seq, d = x.shape
    n_layers = ada_w.shape[0]
    assert seq % MOBA_BLOCK == 0 and seq % FFN_TM == 0 and MOBA_BLOCK % SGU_CHUNK == 0
    assert n_batch <= MOD_ROWS and w_in.shape[2] == N_GROUPS_IN * GROUP_W
    n_blk = seq // MOBA_BLOCK

    c_pad = jnp.zeros((MOD_ROWS, d), F32).at[:n_batch].set(c)
    mod, w13_a, w2_a, w_in_l = _ada_call(c_pad, ada_w, ada_b,
                                         [(ffn1_w13, 0), (ffn1_w2, 0), (w_in, 0)])
    mod = mod[:, :n_batch].reshape(n_layers, n_batch, 9, d)

    x2 = x.reshape(n_batch * seq, d)
    for l in range(n_layers):
        mod_l = mod[l]
        x2 = _ffn_call(x2, mod_l, norm_g[l, 0:1], w13_a, w2_a, 0, seq)
        (yabd, q, k, vt, kmean), (w_out_l, w13_b, w2_b) = _in_mix_call(
            x2, mod_l, norm_g[l, 1:2], w_in_l, pool_w[l],
            pool_scale[l].reshape(1, GROUP_W), sgu_w[l], sgu_b[l].T,
            sgu_norm_g[l].reshape(1, GROUP_W), q_norm_g[l].reshape(1, HEAD_DIM),
            k_norm_g[l].reshape(1, HEAD_DIM), conv_w[l],
            out_norm_g[l].reshape(4, GROUP_W),
            [(w_out, l), (ffn2_w13, l), (ffn2_w2, l)], seq)
        next_casts = ([(ffn1_w13, l + 1), (ffn1_w2, l + 1), (w_in, l + 1)]
                      if l + 1 < n_layers else [])
        x2, next_weights = _moba_out_call(
            q, k.reshape(n_batch, n_blk, MOBA_BLOCK, GROUP_W),
            vt.reshape(n_batch, n_blk, GROUP_W, MOBA_BLOCK),
            kmean.reshape(n_batch, n_blk, GROUP_W), yabd, x2, mod_l,
            out_norm_g[l].reshape(4, GROUP_W), w_out_l, next_casts)
        x2 = _ffn_call(x2, mod_l, norm_g[l, 2:3], w13_b, w2_b, 2, seq)
        if next_weights:
            w13_a, w2_a, w_in_l = next_weights
    return x2.reshape(n_batch, seq, d)
```

```python
import functools

import jax
import jax.numpy as jnp
from jax import lax
from jax.experimental import pallas as pl
from jax.experimental.pallas import tpu as pltpu

F32 = jnp.float32
BF16 = jnp.bfloat16

GROUP_W = 512
HEAD_DIM = 128
N_HEADS = GROUP_W // HEAD_DIM
N_GROUPS_IN = 9
POOL_WINDOWS = (2, 4, 8, 16)
POOL_HALO = 16
SGU_CHUNK = 128
MOBA_BLOCK = 256
MOBA_TOPK = 3
CONV_WIDTH = 3
CONV_HALO = 8
FFN_RES = 0.5
EPS = 1e-6
MOD_ROWS = 8
BF16_ROWS = 16

ADA_TN = 1024
MIB = 1024 * 1024
FFN_TM = 1024
FFN_SUB_ROWS = 512
FFN_TF = 512
OUT_CHUNK = 512
MOBA_VARIANTS = 4
PROJ_CHUNK = 256
NORM_ROWS = 16
NORM_UNROLL = 8
FFN_VMEM_LIMIT = (2 * 2 * FFN_TM * 2048 * 4 + FFN_TM * 2048 * 2 + 2 * 3 * 2048 * FFN_TF * 2
                  + FFN_SUB_ROWS * (3 * FFN_TF + OUT_CHUNK) * 4 + 4 * MIB)


def _rms(x, g):
    ms = jnp.mean(x * x, axis=-1, keepdims=True)
    return x * lax.rsqrt(ms + EPS) * g


def _silu(x):
    return x * (1.0 / (1.0 + jnp.exp(-x)))


def _gelu_tanh(x):
    c = 0.7978845608028654
    return x * (0.5 * (1.0 + jnp.tanh(c * (x + 0.044715 * (x * x * x)))))


def _dot(a, b):
    return jnp.dot(a, b, preferred_element_type=F32)


def _dot_nt(a, b, precision=None):
    return lax.dot_general(a, b, (((1,), (1,)), ((), ())),
                           precision=precision, preferred_element_type=F32)


def _cast_specs(casts, n_steps, step_of):
    in_specs, out_specs, out_shapes = [], [], []
    for w, layer in casts:
        _, rows, cols = w.shape
        n_blocks = max(nb for nb in range(1, n_steps + 1)
                       if rows % nb == 0 and (rows // nb) % BF16_ROWS == 0)

        def block_of(*g, n_blocks=n_blocks):
            return jnp.minimum(step_of(*g), n_blocks - 1)

        in_specs.append(pl.BlockSpec(
            (None, rows // n_blocks, cols),
            lambda *g, block_of=block_of, layer=layer: (layer, block_of(*g), 0)))
        out_specs.append(pl.BlockSpec(
            (rows // n_blocks, cols), lambda *g, block_of=block_of: (block_of(*g), 0)))
        out_shapes.append(jax.ShapeDtypeStruct((rows, cols), BF16))
    return in_specs, out_specs, out_shapes


def _cast_bytes(casts, n_steps):
    specs, _, _ = _cast_specs(casts, n_steps, lambda *g: 0)
    return sum(2 * (4 + 2) * s.block_shape[1] * s.block_shape[2] for s in specs)


def _run_casts(src_refs, dst_refs):
    for src, dst in zip(src_refs, dst_refs):
        dst[...] = src[...].astype(BF16)


def _ada_kernel(c_ref, w_ref, b_ref, *refs, n_cast):
    cast_src, o_ref, cast_dst = refs[:n_cast], refs[n_cast], refs[n_cast + 1:]
    s = _silu(c_ref[...]).astype(BF16)
    o_ref[0] = _dot(s, w_ref[0].astype(BF16)) + b_ref[0]
    _run_casts(cast_src, cast_dst)


def _ada_call(c_pad, ada_w, ada_b, casts):
    n_layers, d, n_out = ada_w.shape
    n_cols = n_out // ADA_TN
    cast_in, cast_out, cast_shapes = _cast_specs(casts, n_layers * n_cols,
                                                 lambda l, n: l * n_cols + n)
    return pl.pallas_call(
        functools.partial(_ada_kernel, n_cast=len(casts)),
        out_shape=[jax.ShapeDtypeStruct((n_layers, MOD_ROWS, n_out), F32)] + cast_shapes,
        grid=(n_layers, n_cols),
        in_specs=[
            pl.BlockSpec((MOD_ROWS, d), lambda l, n: (0, 0)),
            pl.BlockSpec((1, d, ADA_TN), lambda l, n: (l, 0, n)),
            pl.BlockSpec((1, 1, ADA_TN), lambda l, n: (l, 0, n)),
        ] + cast_in,
        out_specs=[pl.BlockSpec((1, MOD_ROWS, ADA_TN), lambda l, n: (l, 0, n))] + cast_out,
        compiler_params=pltpu.CompilerParams(
            dimension_semantics=("arbitrary", "arbitrary"),
            vmem_limit_bytes=(2 * d * ADA_TN * (4 + 2) + _cast_bytes(casts, n_layers * n_cols)
                              + 4 * MIB)),
        name="ada_mod",
    )(c_pad, ada_w, ada_b.reshape(n_layers, 1, n_out), *[w for w, _ in casts])


def _ffn_kernel(x_ref, mod_ref, g_ref, w1_ref, w3_ref, w2_ref, o_ref, h_ref, *, sub):
    j = pl.program_id(1)
    tm, d_out = o_ref.shape

    @pl.when(j == 0)
    def _():
        shift = mod_ref[0, 3 * sub:3 * sub + 1, :]
        gain = g_ref[...] * (1.0 + mod_ref[0, 3 * sub + 1:3 * sub + 2, :])

        def norm_rows(r, carry):
            rows = pl.ds(pl.multiple_of(r * NORM_ROWS, NORM_ROWS), NORM_ROWS)
            x = x_ref[rows, :]
            inv = lax.rsqrt(jnp.mean(x * x, axis=-1, keepdims=True) + EPS)
            h_ref[rows, :] = (x * inv * gain + shift).astype(BF16)
            o_ref[rows, :] = jnp.zeros((NORM_ROWS, d_out), F32)
            return carry

        lax.fori_loop(0, tm // NORM_ROWS, norm_rows, 0, unroll=NORM_UNROLL)

    for r in range(tm // FFN_SUB_ROWS):
        rows = slice(r * FFN_SUB_ROWS, (r + 1) * FFN_SUB_ROWS)
        h = h_ref[rows, :]
        a = _dot(h, w1_ref[...])
        b = _dot(h, w3_ref[...])
        act = (_silu(a) * b).astype(BF16)
        for n in range(d_out // OUT_CHUNK):
            cs = slice(n * OUT_CHUNK, (n + 1) * OUT_CHUNK)
            o_ref[rows, cs] += _dot(act, w2_ref[:, cs])

    @pl.when(j == pl.num_programs(1) - 1)
    def _():
        gate = FFN_RES * mod_ref[0, 3 * sub + 2:3 * sub + 3, :]

        def residual_rows(r, carry):
            rows = pl.ds(pl.multiple_of(r * NORM_ROWS, NORM_ROWS), NORM_ROWS)
            o_ref[rows, :] = x_ref[rows, :] + gate * o_ref[rows, :]
            return carry

        lax.fori_loop(0, tm // NORM_ROWS, residual_rows, 0, unroll=NORM_UNROLL)


def _ffn_call(x2, mod_l, g, w13, w2, sub, seq):
    m, d = x2.shape
    d_ff = w2.shape[0]
    tiles_per_batch = seq // FFN_TM
    n_ff = d_ff // FFN_TF
    return pl.pallas_call(
        functools.partial(_ffn_kernel, sub=sub),
        out_shape=jax.ShapeDtypeStruct((m, d), F32),
        grid=(m // FFN_TM, n_ff),
        in_specs=[
            pl.BlockSpec((FFN_TM, d), lambda i, j: (i, 0)),
            pl.BlockSpec((1, 9, d), lambda i, j: (i // tiles_per_batch, 0, 0)),
            pl.BlockSpec((1, d), lambda i, j: (0, 0)),
            pl.BlockSpec((d, FFN_TF), lambda i, j: (0, j)),
            pl.BlockSpec((d, FFN_TF), lambda i, j: (0, j + n_ff)),
            pl.BlockSpec((FFN_TF, d), lambda i, j: (j, 0)),
        ],
        out_specs=pl.BlockSpec((FFN_TM, d), lambda i, j: (i, 0)),
        scratch_shapes=[pltpu.VMEM((FFN_TM, d), BF16)],
        compiler_params=pltpu.CompilerParams(
            dimension_semantics=("arbitrary", "arbitrary"),
            vmem_limit_bytes=FFN_VMEM_LIMIT),
        name=f"ffn{sub}",
    )(x2, mod_l, g, w13, w13, w2)


def _in_mix_kernel(x_ref, mod_ref, g_ref, win_ref, poolw_ref, pools_ref, sguw_ref,
                   sgubt_ref, sgug_ref, gq_ref, gk_ref, convw_ref, ong_ref, *refs,
                   tiles_per_batch, n_cast):
    cast_src, refs = refs[:n_cast], refs[n_cast:]
    yabd_ref, q_ref, k_ref, vt_ref, kmean_ref = refs[:5]
    cast_dst, (pbuf, zbuf) = refs[5:5 + n_cast], refs[5 + n_cast:]
    _run_casts(cast_src, cast_dst)
    tm = x_ref.shape[0]
    tib = pl.program_id(0) % tiles_per_batch

    @pl.when(tib == 0)
    def _():
        pbuf[0:POOL_HALO, :] = jnp.zeros((POOL_HALO, GROUP_W), F32)
        zbuf[0:CONV_HALO, :] = jnp.zeros((CONV_HALO, GROUP_W), F32)

    shift = mod_ref[0, 3:4, :]
    scale = mod_ref[0, 4:5, :]
    h = (_rms(x_ref[...], g_ref[...]) * (1.0 + scale) + shift).astype(BF16)

    proj = {}

    def project(*groups):
        for g in groups:
            proj[g] = _dot(h, win_ref[:, g * GROUP_W:(g + 1) * GROUP_W])

    def heads():
        return [slice(hd * HEAD_DIM, (hd + 1) * HEAD_DIM) for hd in range(N_HEADS)]

    def pool_mixer():
        pbuf[POOL_HALO:POOL_HALO + tm, :] = proj[0]
        pos = tib * tm + lax.broadcasted_iota(jnp.int32, (tm, 1), 0)
        ya_parts = []
        for hd, (cs, win) in enumerate(zip(heads(), POOL_WINDOWS)):
            xg = pbuf[POOL_HALO:POOL_HALO + tm, cs]
            wsum = xg
            for lag in range(1, win):
                wsum = wsum + pbuf[POOL_HALO - lag:POOL_HALO - lag + tm, cs]
            cnt = jnp.minimum(pos + 1, win).astype(F32)
            dlt = wsum / cnt - xg
            ya_parts.append(_dot(dlt.astype(BF16), poolw_ref[hd].astype(BF16)))
        ya = jnp.concatenate(ya_parts, axis=1) * pools_ref[...]
        pbuf[0:POOL_HALO, :] = pbuf[tm:tm + POOL_HALO, :]
        yabd_ref[:, 0:GROUP_W] = _rms(ya, ong_ref[0:1, :]).astype(BF16)

    def sgu_mixer():
        u = _gelu_tanh(proj[1])
        v = _gelu_tanh(proj[2])
        r_t = lax.broadcasted_iota(jnp.int32, (SGU_CHUNK, SGU_CHUNK), 0)
        r_s = lax.broadcasted_iota(jnp.int32, (SGU_CHUNK, SGU_CHUNK), 1)
        yb_parts = []
        for hd, cs in enumerate(heads()):
            vh = _rms(v[:, cs], sgug_ref[:, cs]).astype(BF16)
            wm = jnp.where(r_t >= r_s, sguw_ref[hd], 0.0).astype(BF16)
            bias = sgubt_ref[:, hd:hd + 1]
            mixed = [_dot(wm, vh[c * SGU_CHUNK:(c + 1) * SGU_CHUNK, :]) + bias
                     for c in range(tm // SGU_CHUNK)]
            yb_parts.append(u[:, cs] * jnp.concatenate(mixed, axis=0))
        yb = jnp.concatenate(yb_parts, axis=1)
        yabd_ref[:, GROUP_W:2 * GROUP_W] = _rms(yb, ong_ref[1:2, :]).astype(BF16)

    def conv_mixer():
        zbuf[CONV_HALO:CONV_HALO + tm, :] = proj[7] * proj[8]
        conv = jnp.zeros((tm, GROUP_W), F32)
        for tap in range(CONV_WIDTH):
            off = CONV_HALO - (CONV_WIDTH - 1) + tap
            conv = conv + convw_ref[tap:tap + 1, :] * zbuf[off:off + tm, :]
        zbuf[0:CONV_HALO, :] = zbuf[tm:tm + CONV_HALO, :]
        yabd_ref[:, 2 * GROUP_W:3 * GROUP_W] = _rms(proj[6] * conv, ong_ref[3:4, :]).astype(BF16)

    def qk_prep():
        qk_scale = HEAD_DIM ** -0.5
        q = jnp.concatenate([_rms(proj[3][:, cs], gq_ref[...]) * qk_scale for cs in heads()],
                            axis=1)
        k = jnp.concatenate([_rms(proj[4][:, cs], gk_ref[...]) for cs in heads()], axis=1)
        q_ref[...] = q
        k_ref[...] = k.astype(BF16)
        kmean_ref[0] = jnp.mean(k, axis=0, keepdims=True)

    project(1, 2, 0, 3, 4)
    sgu_mixer()
    pool_mixer()
    qk_prep()
    project(6, 7, 8, 5)
    conv_mixer()
    vt_ref[0] = proj[5].T.astype(BF16)


def _in_mix_call(x2, mod_l, g, w_in, pool_w, pool_scale, sgu_w, sgu_bt, sgu_g,
                 gq, gk, conv_w, out_g, casts, seq):
    m, d = x2.shape
    tm = MOBA_BLOCK
    n_tiles = m // tm
    tiles_per_batch = seq // tm
    const2 = lambda i: (0, 0)
    const3 = lambda i: (0, 0, 0)
    cast_in, cast_out, cast_shapes = _cast_specs(casts, n_tiles, lambda i: i)
    vmem = (w_in.size * 2 + 2 * tm * d * 4 + 2 * tm * 6 * GROUP_W * 4
            + 2 * N_GROUPS_IN * tm * GROUP_W * 4 + _cast_bytes(casts, n_tiles) + 4 * MIB)
    outs = pl.pallas_call(
        functools.partial(_in_mix_kernel, tiles_per_batch=tiles_per_batch, n_cast=len(casts)),
        out_shape=[
            jax.ShapeDtypeStruct((m, 3 * GROUP_W), BF16),
            jax.ShapeDtypeStruct((m, GROUP_W), F32),
            jax.ShapeDtypeStruct((m, GROUP_W), BF16),
            jax.ShapeDtypeStruct((n_tiles, GROUP_W, tm), BF16),
            jax.ShapeDtypeStruct((n_tiles, 1, GROUP_W), F32),
        ] + cast_shapes,
        grid=(n_tiles,),
        in_specs=[
            pl.BlockSpec((tm, d), lambda i: (i, 0)),
            pl.BlockSpec((1, 9, d), lambda i: (i // tiles_per_batch, 0, 0)),
            pl.BlockSpec((1, d), const2),
            pl.BlockSpec(w_in.shape, const2),
            pl.BlockSpec(pool_w.shape, const3),
            pl.BlockSpec((1, GROUP_W), const2),
            pl.BlockSpec(sgu_w.shape, const3),
            pl.BlockSpec(sgu_bt.shape, const2),
            pl.BlockSpec((1, GROUP_W), const2),
            pl.BlockSpec((1, HEAD_DIM), const2),
            pl.BlockSpec((1, HEAD_DIM), const2),
            pl.BlockSpec(conv_w.shape, const2),
            pl.BlockSpec(out_g.shape, const2),
        ] + cast_in,
        out_specs=[
            pl.BlockSpec((tm, 3 * GROUP_W), lambda i: (i, 0)),
            pl.BlockSpec((tm, GROUP_W), lambda i: (i, 0)),
            pl.BlockSpec((tm, GROUP_W), lambda i: (i, 0)),
            pl.BlockSpec((1, GROUP_W, tm), lambda i: (i, 0, 0)),
            pl.BlockSpec((1, 1, GROUP_W), lambda i: (i, 0, 0)),
        ] + cast_out,
        scratch_shapes=[
            pltpu.VMEM((POOL_HALO + tm, GROUP_W), F32),
            pltpu.VMEM((CONV_HALO + tm, GROUP_W), F32),
        ],
        compiler_params=pltpu.CompilerParams(
            dimension_semantics=("arbitrary",),
            vmem_limit_bytes=vmem),
        name="in_mix",
    )(x2, mod_l, g, w_in, pool_w, pool_scale, sgu_w, sgu_bt, sgu_g, gq, gk, conv_w, out_g,
      *[w for w, _ in casts])
    return outs[:5], outs[5:]


def _moba_out_kernel(q_ref, k_ref, vt_ref, kmean_ref, yabd_ref, x_ref, mod_ref, ong_ref,
                     wout_ref, o_ref, qb_ref, ycn_ref, projl_ref, *, n_tiles):
    t = pl.program_id(0)
    tq = q_ref.shape[0]
    n_blk = kmean_ref.shape[1]
    own = jnp.minimum(t, n_tiles - 1) % n_blk
    neg_inf = -jnp.inf
    heads = [slice(hd * HEAD_DIM, (hd + 1) * HEAD_DIM) for hd in range(N_HEADS)]

    @pl.when(t == 0)
    def _():
        ycn_ref[...] = jnp.zeros_like(ycn_ref)
        projl_ref[...] = jnp.zeros_like(projl_ref)

    def tile(n_visit):
        blk_id = lax.broadcasted_iota(jnp.int32, (n_blk, tq), 0).astype(F32)
        past = blk_id < own.astype(F32)
        key_i = lax.broadcasted_iota(jnp.int32, (tq, tq), 0)
        qry_i = lax.broadcasted_iota(jnp.int32, (tq, tq), 1)

        n_chunks = o_ref.shape[1] // PROJ_CHUNK
        chunks_issued = [0]

        def issue_proj_chunks(units_done):
            while chunks_issued[0] * (n_visit + 1) < units_done * n_chunks:
                cs = slice(chunks_issued[0] * PROJ_CHUNK, (chunks_issued[0] + 1) * PROJ_CHUNK)
                projl_ref[:, cs] = (
                    _dot(yabd_ref[:, 0:2 * GROUP_W], wout_ref[0:2 * GROUP_W, cs])
                    + _dot(yabd_ref[:, 2 * GROUP_W:3 * GROUP_W],
                           wout_ref[3 * GROUP_W:4 * GROUP_W, cs]))
                chunks_issued[0] += 1

        qb_ref[...] = q_ref[...].astype(BF16)
        blk_scores = [_dot_nt(kmean_ref[0, :, cs], q_ref[:, cs], precision=lax.Precision.HIGHEST)
                      for cs in heads]
        own_scores = [_dot_nt(k_ref[0, own, :, cs], qb_ref[:, cs]) for cs in heads]

        def block_scores(nb):
            return [_dot_nt(k_ref[0, nb, :, cs], qb_ref[:, cs]) for cs in heads]

        scores_next = block_scores(0) if n_visit else None

        prev_proj = projl_ref[...] + _dot(ycn_ref[...], wout_ref[2 * GROUP_W:3 * GROUP_W, :])
        o_ref[...] = x_ref[...] + mod_ref[0, 5:6, :] * prev_proj

        issue_proj_chunks(1)
        sel, m, l, acc = [], [], [], []
        for hd, cs in enumerate(heads):
            sb = jnp.where(past, blk_scores[hd], neg_inf)
            chosen = jnp.zeros((n_blk, tq), F32)
            for _ in range(min(MOBA_TOPK, n_blk)):
                top = jnp.max(sb, axis=0, keepdims=True)
                idx = jnp.min(jnp.where(sb == top, blk_id, float(n_blk)), axis=0, keepdims=True)
                hit = blk_id == idx
                chosen = jnp.where(hit, 1.0, chosen)
                sb = jnp.where(hit, neg_inf, sb)
            sel.append(jnp.where(past, chosen, 0.0))

            s = jnp.where(key_i <= qry_i, own_scores[hd], neg_inf)
            m.append(jnp.max(s, axis=0, keepdims=True))
            p = jnp.exp(s - m[hd])
            l.append(jnp.sum(p, axis=0, keepdims=True))
            acc.append(_dot(vt_ref[0, own, cs, :], p.astype(BF16)))

        for nb in range(n_visit):
            scores = scores_next
            if nb + 1 < n_visit:
                scores_next = block_scores(nb + 1)
            issue_proj_chunks(nb + 2)
            for hd, cs in enumerate(heads):
                s = jnp.where(sel[hd][nb:nb + 1, :] > 0.0, scores[hd], neg_inf)
                m_new = jnp.maximum(m[hd], jnp.max(s, axis=0, keepdims=True))
                alpha = jnp.exp(m[hd] - m_new)
                p = jnp.exp(s - m_new)
                l[hd] = alpha * l[hd] + jnp.sum(p, axis=0, keepdims=True)
                acc[hd] = alpha * acc[hd] + _dot(vt_ref[0, nb, cs, :], p.astype(BF16))
                m[hd] = m_new

        yc = jnp.concatenate([(acc[hd] / l[hd]).T for hd in range(N_HEADS)], axis=1)
        ycn_ref[...] = _rms(yc, ong_ref[2:3, :]).astype(BF16)

    step = n_blk // MOBA_VARIANTS
    for v in range(MOBA_VARIANTS):
        in_range = jnp.logical_and(own >= v * step, own < (v + 1) * step)
        pl.when(in_range)(functools.partial(tile, (v + 1) * step - 1))


def _moba_out_call(q, k4, vt4, kmean, yabd, x2, mod_l, out_g, w_out):
    m, d = x2.shape
    n_batch, n_blk, tq, _ = k4.shape
    n_tiles = n_batch * n_blk
    cur = lambda t: jnp.minimum(t, n_tiles - 1)
    prev = lambda t: jnp.maximum(t - 1, 0)
    vmem = (2 * 2 * n_blk * tq * GROUP_W * 2 + w_out.size * 2 + 2 * 2 * tq * d * 4
            + 2 * tq * GROUP_W * (4 + 3 * 2) + 3 * tq * d * 4 + 6 * N_HEADS * tq * tq * 4
            + 2 * MIB)
    return pl.pallas_call(
        functools.partial(_moba_out_kernel, n_tiles=n_tiles),
        out_shape=jax.ShapeDtypeStruct((m, d), F32),
        grid=(n_tiles + 1,),
        in_specs=[
            pl.BlockSpec((tq, GROUP_W), lambda t: (cur(t), 0)),
            pl.BlockSpec((1, n_blk, tq, GROUP_W), lambda t: (cur(t) // n_blk, 0, 0, 0)),
            pl.BlockSpec((1, n_blk, GROUP_W, tq), lambda t: (cur(t) // n_blk, 0, 0, 0)),
            pl.BlockSpec((1, n_blk, GROUP_W), lambda t: (cur(t) // n_blk, 0, 0)),
            pl.BlockSpec((tq, 3 * GROUP_W), lambda t: (cur(t), 0)),
            pl.BlockSpec((tq, d), lambda t: (prev(t), 0)),
            pl.BlockSpec((1, 9, d), lambda t: (prev(t) // n_blk, 0, 0)),
            pl.BlockSpec(out_g.shape, lambda t: (0, 0)),
            pl.BlockSpec(w_out.shape, lambda t: (0, 0)),
        ],
        out_specs=pl.BlockSpec((tq, d), lambda t: (prev(t), 0)),
        scratch_shapes=[
            pltpu.VMEM((tq, GROUP_W), BF16),
            pltpu.VMEM((tq, GROUP_W), BF16),
            pltpu.VMEM((tq, d), F32),
        ],
        compiler_params=pltpu.CompilerParams(
            dimension_semantics=("arbitrary",),
            vmem_limit_bytes=vmem),
        name="moba_out",
    )(q, k4, vt4, kmean, yabd, x2, mod_l, out_g, w_out)


def kernel(x, c, ada_w, ada_b, norm_g, ffn1_w13, ffn1_w2, w_in, pool_w, pool_scale, sgu_w,
           sgu_b, sgu_norm_g, q_norm_g, k_norm_g, conv_w, out_norm_g, w_out, ffn2_w13, ffn2_w2):
    n_batch, seq, d = x.shape
    n_layers = ada_w.shape[0]
    assert seq % MOBA_BLOCK == 0 and seq % FFN_TM == 0 and MOBA_BLOCK % SGU_CHUNK == 0
    assert n_batch <= MOD_ROWS and w_in.shape[2] == N_GROUPS_IN * GROUP_W
    n_blk = seq // MOBA_BLOCK

    c_pad = jnp.zeros((MOD_ROWS, d), F32).at[:n_batch].set(c)
    mod, w13_a, w2_a, w_in_l = _ada_call(c_pad, ada_w, ada_b,
                                         [(ffn1_w13, 0), (ffn1_w2, 0), (w_in, 0)])
    mod = mod[:, :n_batch].reshape(n_layers, n_batch, 9, d)

    x2 = x.reshape(n_batch * seq, d)
    for l in range(n_layers):
        mod_l = mod[l]
        x2 = _ffn_call(x2, mod_l, norm_g[l, 0:1], w13_a, w2_a, 0, seq)
        casts = [(w_out, l), (ffn2_w13, l), (ffn2_w2, l)]
        if l + 1 < n_layers:
            casts += [(ffn1_w13, l + 1), (ffn1_w2, l + 1), (w_in, l + 1)]
        (yabd, q, k, vt, kmean), cast_weights = _in_mix_call(
            x2, mod_l, norm_g[l, 1:2], w_in_l, pool_w[l],
            pool_scale[l].reshape(1, GROUP_W), sgu_w[l], sgu_b[l].T,
            sgu_norm_g[l].reshape(1, GROUP_W), q_norm_g[l].reshape(1, HEAD_DIM),
            k_norm_g[l].reshape(1, HEAD_DIM), conv_w[l],
            out_norm_g[l].reshape(4, GROUP_W), casts, seq)
        w_out_l, w13_b, w2_b = cast_weights[:3]
        x2 = _moba_out_call(
            q, k.reshape(n_batch, n_blk, MOBA_BLOCK, GROUP_W),
            vt.reshape(n_batch, n_blk, GROUP_W, MOBA_BLOCK),
            kmean.reshape(n_batch, n_blk, GROUP_W), yabd, x2, mod_l,
            out_norm_g[l].reshape(4, GROUP_W), w_out_l)
        x2 = _ffn_call(x2, mod_l, norm_g[l, 2:3], w13_b, w2_b, 2, seq)
        if l + 1 < n_layers:
            w13_a, w2_a, w_in_l = cast_weights[3:]
    return x2.reshape(n_batch, seq, d)
```

```python
import functools

import jax
import jax.numpy as jnp
from jax import lax
from jax.experimental import pallas as pl
from jax.experimental.pallas import tpu as pltpu

F32 = jnp.float32
BF16 = jnp.bfloat16

GROUP_W = 512
HEAD_DIM = 128
N_HEADS = GROUP_W // HEAD_DIM
N_GROUPS_IN = 9
POOL_WINDOWS = (2, 4, 8, 16)
POOL_HALO = 16
SGU_CHUNK = 128
MOBA_BLOCK = 256
MOBA_TOPK = 3
CONV_WIDTH = 3
CONV_HALO = 8
FFN_RES = 0.5
EPS = 1e-6
MOD_ROWS = 8
BF16_ROWS = 16

ADA_TN = 1024
MIB = 1024 * 1024
FFN_TM = 1024
FFN_SUB_ROWS = 512
FFN_TF = 512
OUT_CHUNK = 512
MOBA_VARIANTS = 4
PROJ_CHUNK = 256
NORM_ROWS = 16
NORM_UNROLL = 8
FFN_VMEM_LIMIT = (2 * 2 * FFN_TM * 2048 * 4 + FFN_TM * 2048 * 2 + 2 * 3 * 2048 * FFN_TF * 2
                  + FFN_SUB_ROWS * (3 * FFN_TF + OUT_CHUNK) * 4 + 4 * MIB)


def _rms(x, g):
    ms = jnp.mean(x * x, axis=-1, keepdims=True)
    return x * lax.rsqrt(ms + EPS) * g


def _silu(x):
    return x * (1.0 / (1.0 + jnp.exp(-x)))


def _gelu_tanh(x):
    c = 0.7978845608028654
    return x * (0.5 * (1.0 + jnp.tanh(c * (x + 0.044715 * (x * x * x)))))


def _dot(a, b):
    return jnp.dot(a, b, preferred_element_type=F32)


def _dot_nt(a, b, precision=None):
    return lax.dot_general(a, b, (((1,), (1,)), ((), ())),
                           precision=precision, preferred_element_type=F32)


def _cast_specs(casts, n_steps, step_of):
    in_specs, out_specs, out_shapes = [], [], []
    for w, layer in casts:
        _, rows, cols = w.shape
        n_blocks = max(nb for nb in range(1, n_steps + 1)
                       if rows % nb == 0 and (rows // nb) % BF16_ROWS == 0)

        def block_of(*g, n_blocks=n_blocks):
            return jnp.minimum(step_of(*g), n_blocks - 1)

        in_specs.append(pl.BlockSpec(
            (None, rows // n_blocks, cols),
            lambda *g, block_of=block_of, layer=layer: (layer, block_of(*g), 0)))
        out_specs.append(pl.BlockSpec(
            (rows // n_blocks, cols), lambda *g, block_of=block_of: (block_of(*g), 0)))
        out_shapes.append(jax.ShapeDtypeStruct((rows, cols), BF16))
    return in_specs, out_specs, out_shapes


def _cast_bytes(casts, n_steps):
    specs, _, _ = _cast_specs(casts, n_steps, lambda *g: 0)
    return sum(2 * (4 + 2) * s.block_shape[1] * s.block_shape[2] for s in specs)


def _run_casts(src_refs, dst_refs):
    for src, dst in zip(src_refs, dst_refs):
        dst[...] = src[...].astype(BF16)


def _ada_kernel(c_ref, w_ref, b_ref, *refs, n_cast):
    cast_src, o_ref, cast_dst = refs[:n_cast], refs[n_cast], refs[n_cast + 1:]
    s = _silu(c_ref[...]).astype(BF16)
    o_ref[0] = _dot(s, w_ref[0].astype(BF16)) + b_ref[0]
    _run_casts(cast_src, cast_dst)


def _ada_call(c_pad, ada_w, ada_b, casts):
    n_layers, d, n_out = ada_w.shape
    n_cols = n_out // ADA_TN
    cast_in, cast_out, cast_shapes = _cast_specs(casts, n_layers * n_cols,
                                                 lambda l, n: l * n_cols + n)
    return pl.pallas_call(
        functools.partial(_ada_kernel, n_cast=len(casts)),
        out_shape=[jax.ShapeDtypeStruct((n_layers, MOD_ROWS, n_out), F32)] + cast_shapes,
        grid=(n_layers, n_cols),
        in_specs=[
            pl.BlockSpec((MOD_ROWS, d), lambda l, n: (0, 0)),
            pl.BlockSpec((1, d, ADA_TN), lambda l, n: (l, 0, n)),
            pl.BlockSpec((1, 1, ADA_TN), lambda l, n: (l, 0, n)),
        ] + cast_in,
        out_specs=[pl.BlockSpec((1, MOD_ROWS, ADA_TN), lambda l, n: (l, 0, n))] + cast_out,
        compiler_params=pltpu.CompilerParams(
            dimension_semantics=("arbitrary", "arbitrary"),
            vmem_limit_bytes=(2 * d * ADA_TN * (4 + 2) + _cast_bytes(casts, n_layers * n_cols)
                              + 4 * MIB)),
        name="ada_mod",
    )(c_pad, ada_w, ada_b.reshape(n_layers, 1, n_out), *[w for w, _ in casts])


def _ffn_kernel(x_ref, mod_ref, g_ref, w1_ref, w3_ref, w2_ref, *refs, sub, n_cast):
    cast_src, o_ref = refs[:n_cast], refs[n_cast]
    cast_dst, h_ref = refs[n_cast + 1:2 * n_cast + 1], refs[2 * n_cast + 1]
    _run_casts(cast_src, cast_dst)
    j = pl.program_id(1)
    tm, d_out = o_ref.shape

    @pl.when(j == 0)
    def _():
        shift = mod_ref[0, 3 * sub:3 * sub + 1, :]
        gain = g_ref[...] * (1.0 + mod_ref[0, 3 * sub + 1:3 * sub + 2, :])

        def norm_rows(r, carry):
            rows = pl.ds(pl.multiple_of(r * NORM_ROWS, NORM_ROWS), NORM_ROWS)
            x = x_ref[rows, :]
            o_ref[rows, :] = x
            inv = lax.rsqrt(jnp.mean(x * x, axis=-1, keepdims=True) + EPS)
            h_ref[rows, :] = (x * inv * gain + shift).astype(BF16)
            return carry

        lax.fori_loop(0, tm // NORM_ROWS, norm_rows, 0, unroll=NORM_UNROLL)

    gate = FFN_RES * mod_ref[0, 3 * sub + 2:3 * sub + 3, :]
    for r in range(tm // FFN_SUB_ROWS):
        rows = slice(r * FFN_SUB_ROWS, (r + 1) * FFN_SUB_ROWS)
        h = h_ref[rows, :]
        a = _dot(h, w1_ref[...])
        b = _dot(h, w3_ref[...])
        act = (_silu(a) * b).astype(BF16)
        for n in range(d_out // OUT_CHUNK):
            cs = slice(n * OUT_CHUNK, (n + 1) * OUT_CHUNK)
            o_ref[rows, cs] += gate[:, cs] * _dot(act, w2_ref[:, cs])


def _ffn_call(x2, mod_l, g, w13, w2, sub, seq, casts=()):
    m, d = x2.shape
    d_ff = w2.shape[0]
    tiles_per_batch = seq // FFN_TM
    n_ff = d_ff // FFN_TF
    n_steps = (m // FFN_TM) * n_ff
    cast_in, cast_out, cast_shapes = _cast_specs(casts, n_steps, lambda i, j: i * n_ff + j)
    outs = pl.pallas_call(
        functools.partial(_ffn_kernel, sub=sub, n_cast=len(casts)),
        out_shape=[jax.ShapeDtypeStruct((m, d), F32)] + cast_shapes,
        grid=(m // FFN_TM, n_ff),
        in_specs=[
            pl.BlockSpec((FFN_TM, d), lambda i, j: (i, 0)),
            pl.BlockSpec((1, 9, d), lambda i, j: (i // tiles_per_batch, 0, 0)),
            pl.BlockSpec((1, d), lambda i, j: (0, 0)),
            pl.BlockSpec((d, FFN_TF), lambda i, j: (0, j)),
            pl.BlockSpec((d, FFN_TF), lambda i, j: (0, j + n_ff)),
            pl.BlockSpec((FFN_TF, d), lambda i, j: (j, 0)),
        ] + cast_in,
        out_specs=[pl.BlockSpec((FFN_TM, d), lambda i, j: (i, 0))] + cast_out,
        scratch_shapes=[pltpu.VMEM((FFN_TM, d), BF16)],
        compiler_params=pltpu.CompilerParams(
            dimension_semantics=("arbitrary", "arbitrary"),
            vmem_limit_bytes=FFN_VMEM_LIMIT + _cast_bytes(casts, n_steps)),
        name=f"ffn{sub}",
    )(x2, mod_l, g, w13, w13, w2, *[w for w, _ in casts])
    return outs[0], outs[1:]


def _in_mix_kernel(x_ref, mod_ref, g_ref, win_ref, poolw_ref, pools_ref, sguw_ref,
                   sgubt_ref, sgug_ref, gq_ref, gk_ref, convw_ref, ong_ref, *refs,
                   tiles_per_batch, n_cast):
    cast_src, refs = refs[:n_cast], refs[n_cast:]
    yabd_ref, q_ref, k_ref, vt_ref, kmean_ref = refs[:5]
    cast_dst, (pbuf, zbuf) = refs[5:5 + n_cast], refs[5 + n_cast:]
    _run_casts(cast_src, cast_dst)
    tm = x_ref.shape[0]
    tib = pl.program_id(0) % tiles_per_batch

    @pl.when(tib == 0)
    def _():
        pbuf[0:POOL_HALO, :] = jnp.zeros((POOL_HALO, GROUP_W), F32)
        zbuf[0:CONV_HALO, :] = jnp.zeros((CONV_HALO, GROUP_W), F32)

    shift = mod_ref[0, 3:4, :]
    scale = mod_ref[0, 4:5, :]
    h = (_rms(x_ref[...], g_ref[...]) * (1.0 + scale) + shift).astype(BF16)

    proj = {}

    def project(*groups):
        for g in groups:
            proj[g] = _dot(h, win_ref[:, g * GROUP_W:(g + 1) * GROUP_W])

    def heads():
        return [slice(hd * HEAD_DIM, (hd + 1) * HEAD_DIM) for hd in range(N_HEADS)]

    def pool_mixer():
        pbuf[POOL_HALO:POOL_HALO + tm, :] = proj[0]
        pos = tib * tm + lax.broadcasted_iota(jnp.int32, (tm, 1), 0)
        ya_parts = []
        for hd, (cs, win) in enumerate(zip(heads(), POOL_WINDOWS)):
            xg = pbuf[POOL_HALO:POOL_HALO + tm, cs]
            wsum = xg
            for lag in range(1, win):
                wsum = wsum + pbuf[POOL_HALO - lag:POOL_HALO - lag + tm, cs]
            cnt = jnp.minimum(pos + 1, win).astype(F32)
            dlt = wsum / cnt - xg
            ya_parts.append(_dot(dlt.astype(BF16), poolw_ref[hd].astype(BF16)))
        ya = jnp.concatenate(ya_parts, axis=1) * pools_ref[...]
        pbuf[0:POOL_HALO, :] = pbuf[tm:tm + POOL_HALO, :]
        yabd_ref[:, 0:GROUP_W] = _rms(ya, ong_ref[0:1, :]).astype(BF16)

    def sgu_mixer():
        u = _gelu_tanh(proj[1])
        v = _gelu_tanh(proj[2])
        r_t = lax.broadcasted_iota(jnp.int32, (SGU_CHUNK, SGU_CHUNK), 0)
        r_s = lax.broadcasted_iota(jnp.int32, (SGU_CHUNK, SGU_CHUNK), 1)
        yb_parts = []
        for hd, cs in enumerate(heads()):
            vh = _rms(v[:, cs], sgug_ref[:, cs]).astype(BF16)
            wm = jnp.where(r_t >= r_s, sguw_ref[hd], 0.0).astype(BF16)
            bias = sgubt_ref[:, hd:hd + 1]
            mixed = [_dot(wm, vh[c * SGU_CHUNK:(c + 1) * SGU_CHUNK, :]) + bias
                     for c in range(tm // SGU_CHUNK)]
            yb_parts.append(u[:, cs] * jnp.concatenate(mixed, axis=0))
        yb = jnp.concatenate(yb_parts, axis=1)
        yabd_ref[:, GROUP_W:2 * GROUP_W] = _rms(yb, ong_ref[1:2, :]).astype(BF16)

    def conv_mixer():
        zbuf[CONV_HALO:CONV_HALO + tm, :] = proj[7] * proj[8]
        conv = jnp.zeros((tm, GROUP_W), F32)
        for tap in range(CONV_WIDTH):
            off = CONV_HALO - (CONV_WIDTH - 1) + tap
            conv = conv + convw_ref[tap:tap + 1, :] * zbuf[off:off + tm, :]
        zbuf[0:CONV_HALO, :] = zbuf[tm:tm + CONV_HALO, :]
        yabd_ref[:, 2 * GROUP_W:3 * GROUP_W] = _rms(proj[6] * conv, ong_ref[3:4, :]).astype(BF16)

    def qk_prep():
        qk_scale = HEAD_DIM ** -0.5
        q = jnp.concatenate([_rms(proj[3][:, cs], gq_ref[...]) * qk_scale for cs in heads()],
                            axis=1)
        k = jnp.concatenate([_rms(proj[4][:, cs], gk_ref[...]) for cs in heads()], axis=1)
        q_ref[...] = q
        k_ref[...] = k.astype(BF16)
        kmean_ref[0] = jnp.mean(k, axis=0, keepdims=True)

    project(1, 2, 0, 3, 4)
    sgu_mixer()
    pool_mixer()
    qk_prep()
    project(6, 7, 8, 5)
    conv_mixer()
    vt_ref[0] = proj[5].T.astype(BF16)


def _in_mix_call(x2, mod_l, g, w_in, pool_w, pool_scale, sgu_w, sgu_bt, sgu_g,
                 gq, gk, conv_w, out_g, casts, seq):
    m, d = x2.shape
    tm = MOBA_BLOCK
    n_tiles = m // tm
    tiles_per_batch = seq // tm
    const2 = lambda i: (0, 0)
    const3 = lambda i: (0, 0, 0)
    cast_in, cast_out, cast_shapes = _cast_specs(casts, n_tiles, lambda i: i)
    vmem = (w_in.size * 2 + 2 * tm * d * 4 + 2 * tm * 6 * GROUP_W * 4
            + 2 * N_GROUPS_IN * tm * GROUP_W * 4 + _cast_bytes(casts, n_tiles) + 4 * MIB)
    outs = pl.pallas_call(
        functools.partial(_in_mix_kernel, tiles_per_batch=tiles_per_batch, n_cast=len(casts)),
        out_shape=[
            jax.ShapeDtypeStruct((m, 3 * GROUP_W), BF16),
            jax.ShapeDtypeStruct((m, GROUP_W), F32),
            jax.ShapeDtypeStruct((m, GROUP_W), BF16),
            jax.ShapeDtypeStruct((n_tiles, GROUP_W, tm), BF16),
            jax.ShapeDtypeStruct((n_tiles, 1, GROUP_W), F32),
        ] + cast_shapes,
        grid=(n_tiles,),
        in_specs=[
            pl.BlockSpec((tm, d), lambda i: (i, 0)),
            pl.BlockSpec((1, 9, d), lambda i: (i // tiles_per_batch, 0, 0)),
            pl.BlockSpec((1, d), const2),
            pl.BlockSpec(w_in.shape, const2),
            pl.BlockSpec(pool_w.shape, const3),
            pl.BlockSpec((1, GROUP_W), const2),
            pl.BlockSpec(sgu_w.shape, const3),
            pl.BlockSpec(sgu_bt.shape, const2),
            pl.BlockSpec((1, GROUP_W), const2),
            pl.BlockSpec((1, HEAD_DIM), const2),
            pl.BlockSpec((1, HEAD_DIM), const2),
            pl.BlockSpec(conv_w.shape, const2),
            pl.BlockSpec(out_g.shape, const2),
        ] + cast_in,
        out_specs=[
            pl.BlockSpec((tm, 3 * GROUP_W), lambda i: (i, 0)),
            pl.BlockSpec((tm, GROUP_W), lambda i: (i, 0)),
            pl.BlockSpec((tm, GROUP_W), lambda i: (i, 0)),
            pl.BlockSpec((1, GROUP_W, tm), lambda i: (i, 0, 0)),
            pl.BlockSpec((1, 1, GROUP_W), lambda i: (i, 0, 0)),
        ] + cast_out,
        scratch_shapes=[
            pltpu.VMEM((POOL_HALO + tm, GROUP_W), F32),
            pltpu.VMEM((CONV_HALO + tm, GROUP_W), F32),
        ],
        compiler_params=pltpu.CompilerParams(
            dimension_semantics=("arbitrary",),
            vmem_limit_bytes=vmem),
        name="in_mix",
    )(x2, mod_l, g, w_in, pool_w, pool_scale, sgu_w, sgu_bt, sgu_g, gq, gk, conv_w, out_g,
      *[w for w, _ in casts])
    return outs[:5], outs[5:]


def _moba_out_kernel(q_ref, k_ref, vt_ref, kmean_ref, yabd_ref, x_ref, mod_ref, ong_ref,
                     wout_ref, o_ref, qb_ref, ycn_ref, projl_ref, *, n_tiles):
    t = pl.program_id(0)
    tq = q_ref.shape[0]
    n_blk = kmean_ref.shape[1]
    own = jnp.minimum(t, n_tiles - 1) % n_blk
    neg_inf = -jnp.inf
    heads = [slice(hd * HEAD_DIM, (hd + 1) * HEAD_DIM) for hd in range(N_HEADS)]

    @pl.when(t == 0)
    def _():
        ycn_ref[...] = jnp.zeros_like(ycn_ref)
        projl_ref[...] = jnp.zeros_like(projl_ref)

    def tile(n_visit):
        blk_id = lax.broadcasted_iota(jnp.int32, (n_blk, tq), 0).astype(F32)
        past = blk_id < own.astype(F32)
        key_i = lax.broadcasted_iota(jnp.int32, (tq, tq), 0)
        qry_i = lax.broadcasted_iota(jnp.int32, (tq, tq), 1)

        n_chunks = o_ref.shape[1] // PROJ_CHUNK
        chunks_issued = [0]

        def issue_proj_chunks(units_done):
            while chunks_issued[0] * (n_visit + 1) < units_done * n_chunks:
                cs = slice(chunks_issued[0] * PROJ_CHUNK, (chunks_issued[0] + 1) * PROJ_CHUNK)
                projl_ref[:, cs] = (
                    _dot(yabd_ref[:, 0:2 * GROUP_W], wout_ref[0:2 * GROUP_W, cs])
                    + _dot(yabd_ref[:, 2 * GROUP_W:3 * GROUP_W],
                           wout_ref[3 * GROUP_W:4 * GROUP_W, cs]))
                chunks_issued[0] += 1

        qb_ref[...] = q_ref[...].astype(BF16)
        blk_scores = [_dot_nt(kmean_ref[0, :, cs], q_ref[:, cs], precision=lax.Precision.HIGHEST)
                      for cs in heads]
        own_scores = [_dot_nt(k_ref[0, own, :, cs], qb_ref[:, cs]) for cs in heads]

        def block_scores(nb):
            return [_dot_nt(k_ref[0, nb, :, cs], qb_ref[:, cs]) for cs in heads]

        scores_next = block_scores(0) if n_visit else None

        prev_proj = projl_ref[...] + _dot(ycn_ref[...], wout_ref[2 * GROUP_W:3 * GROUP_W, :])
        o_ref[...] = x_ref[...] + mod_ref[0, 5:6, :] * prev_proj

        issue_proj_chunks(1)
        sel, m, l, acc = [], [], [], []
        for hd, cs in enumerate(heads):
            sb = jnp.where(past, blk_scores[hd], neg_inf)
            chosen = jnp.zeros((n_blk, tq), F32)
            for _ in range(min(MOBA_TOPK, n_blk)):
                top = jnp.max(sb, axis=0, keepdims=True)
                idx = jnp.min(jnp.where(sb == top, blk_id, float(n_blk)), axis=0, keepdims=True)
                hit = blk_id == idx
                chosen = jnp.where(hit, 1.0, chosen)
                sb = jnp.where(hit, neg_inf, sb)
            sel.append(jnp.where(past, chosen, 0.0))

            s = jnp.where(key_i <= qry_i, own_scores[hd], neg_inf)
            m.append(jnp.max(s, axis=0, keepdims=True))
            p = jnp.exp(s - m[hd])
            l.append(jnp.sum(p, axis=0, keepdims=True))
            acc.append(_dot(vt_ref[0, own, cs, :], p.astype(BF16)))

        for nb in range(n_visit):
            scores = scores_next
            if nb + 1 < n_visit:
                scores_next = block_scores(nb + 1)
            issue_proj_chunks(nb + 2)
            for hd, cs in enumerate(heads):
                s = jnp.where(sel[hd][nb:nb + 1, :] > 0.0, scores[hd], neg_inf)
                m_new = jnp.maximum(m[hd], jnp.max(s, axis=0, keepdims=True))
                alpha = jnp.exp(m[hd] - m_new)
                p = jnp.exp(s - m_new)
                l[hd] = alpha * l[hd] + jnp.sum(p, axis=0, keepdims=True)
                acc[hd] = alpha * acc[hd] + _dot(vt_ref[0, nb, cs, :], p.astype(BF16))
                m[hd] = m_new

        yc = jnp.concatenate([(acc[hd] / l[hd]).T for hd in range(N_HEADS)], axis=1)
        ycn_ref[...] = _rms(yc, ong_ref[2:3, :]).astype(BF16)

    step = n_blk // MOBA_VARIANTS
    for v in range(MOBA_VARIANTS):
        in_range = jnp.logical_and(own >= v * step, own < (v + 1) * step)
        pl.when(in_range)(functools.partial(tile, (v + 1) * step - 1))


def _moba_out_call(q, k4, vt4, kmean, yabd, x2, mod_l, out_g, w_out):
    m, d = x2.shape
    n_batch, n_blk, tq, _ = k4.shape
    n_tiles = n_batch * n_blk
    cur = lambda t: jnp.minimum(t, n_tiles - 1)
    prev = lambda t: jnp.maximum(t - 1, 0)
    vmem = (2 * 2 * n_blk * tq * GROUP_W * 2 + w_out.size * 2 + 2 * 2 * tq * d * 4
            + 2 * tq * GROUP_W * (4 + 3 * 2) + 3 * tq * d * 4 + 6 * N_HEADS * tq * tq * 4
            + 2 * MIB)
    return pl.pallas_call(
        functools.partial(_moba_out_kernel, n_tiles=n_tiles),
        out_shape=jax.ShapeDtypeStruct((m, d), F32),
        grid=(n_tiles + 1,),
        in_specs=[
            pl.BlockSpec((tq, GROUP_W), lambda t: (cur(t), 0)),
            pl.BlockSpec((1, n_blk, tq, GROUP_W), lambda t: (cur(t) // n_blk, 0, 0, 0)),
            pl.BlockSpec((1, n_blk, GROUP_W, tq), lambda t: (cur(t) // n_blk, 0, 0, 0)),
            pl.BlockSpec((1, n_blk, GROUP_W), lambda t: (cur(t) // n_blk, 0, 0)),
            pl.BlockSpec((tq, 3 * GROUP_W), lambda t: (cur(t), 0)),
            pl.BlockSpec((tq, d), lambda t: (prev(t), 0)),
            pl.BlockSpec((1, 9, d), lambda t: (prev(t) // n_blk, 0, 0)),
            pl.BlockSpec(out_g.shape, lambda t: (0, 0)),
            pl.BlockSpec(w_out.shape, lambda t: (0, 0)),
        ],
        out_specs=pl.BlockSpec((tq, d), lambda t: (prev(t), 0)),
        scratch_shapes=[
            pltpu.VMEM((tq, GROUP_W), BF16),
            pltpu.VMEM((tq, GROUP_W), BF16),
            pltpu.VMEM((tq, d), F32),
        ],
        compiler_params=pltpu.CompilerParams(
            dimension_semantics=("arbitrary",),
            vmem_limit_bytes=vmem),
        name="moba_out",
    )(q, k4, vt4, kmean, yabd, x2, mod_l, out_g, w_out)


def kernel(x, c, ada_w, ada_b, norm_g, ffn1_w13, ffn1_w2, w_in, pool_w, pool_scale, sgu_w,
           sgu_b, sgu_norm_g, q_norm_g, k_norm_g, conv_w, out_norm_g, w_out, ffn2_w13, ffn2_w2):
    n_batch, seq, d = x.shape
    n_layers = ada_w.shape[0]
    assert seq % MOBA_BLOCK == 0 and seq % FFN_TM == 0 and MOBA_BLOCK % SGU_CHUNK == 0
    assert n_batch <= MOD_ROWS and w_in.shape[2] == N_GROUPS_IN * GROUP_W
    n_blk = seq // MOBA_BLOCK

    c_pad = jnp.zeros((MOD_ROWS, d), F32).at[:n_batch].set(c)
    mod, w13_a, w2_a = _ada_call(c_pad, ada_w, ada_b, [(ffn1_w13, 0), (ffn1_w2, 0)])
    mod = mod[:, :n_batch].reshape(n_layers, n_batch, 9, d)
    w_in_l = None

    x2 = x.reshape(n_batch * seq, d)
    for l in range(n_layers):
        mod_l = mod[l]
        x2, cast_weights = _ffn_call(x2, mod_l, norm_g[l, 0:1], w13_a, w2_a, 0, seq,
                                     [(w_in, l)] if w_in_l is None else [])
        if cast_weights:
            w_in_l, = cast_weights
        (yabd, q, k, vt, kmean), (w_out_l, w13_b, w2_b) = _in_mix_call(
            x2, mod_l, norm_g[l, 1:2], w_in_l, pool_w[l],
            pool_scale[l].reshape(1, GROUP_W), sgu_w[l], sgu_b[l].T,
            sgu_norm_g[l].reshape(1, GROUP_W), q_norm_g[l].reshape(1, HEAD_DIM),
            k_norm_g[l].reshape(1, HEAD_DIM), conv_w[l],
            out_norm_g[l].reshape(4, GROUP_W),
            [(w_out, l), (ffn2_w13, l), (ffn2_w2, l)], seq)
        x2 = _moba_out_call(
            q, k.reshape(n_batch, n_blk, MOBA_BLOCK, GROUP_W),
            vt.reshape(n_batch, n_blk, GROUP_W, MOBA_BLOCK),
            kmean.reshape(n_batch, n_blk, GROUP_W), yabd, x2, mod_l,
            out_norm_g[l].reshape(4, GROUP_W), w_out_l)
        next_casts = ([(ffn1_w13, l + 1), (ffn1_w2, l + 1), (w_in, l + 1)]
                      if l + 1 < n_layers else [])
        x2, cast_weights = _ffn_call(x2, mod_l, norm_g[l, 2:3], w13_b, w2_b, 2, seq, next_casts)
        if cast_weights:
            w13_a, w2_a, w_in_l = cast_weights
    return x2.reshape(n_batch, seq, d)
```

```python
import functools

import jax
import jax.numpy as jnp
from jax import lax
from jax.experimental import pallas as pl
from jax.experimental.pallas import tpu as pltpu

F32 = jnp.float32
BF16 = jnp.bfloat16

GROUP_W = 512
HEAD_DIM = 128
N_HEADS = GROUP_W // HEAD_DIM
N_GROUPS_IN = 9
POOL_WINDOWS = (2, 4, 8, 16)
POOL_HALO = 16
SGU_CHUNK = 128
MOBA_BLOCK = 256
MOBA_TOPK = 3
CONV_WIDTH = 3
CONV_HALO = 8
FFN_RES = 0.5
EPS = 1e-6
MOD_ROWS = 8
BF16_ROWS = 16

MIB = 1024 * 1024
VMEM_SLACK = 4 * MIB
ADA_TN = 512
FFN_TM = 1024
FFN_SUB_ROWS = 1024
FFN_TF = 512
OUT_CHUNK = 512
MOBA_VARIANTS = 4
PROJ_CHUNK = 256
NORM_ROWS = 16
NORM_UNROLL = 8


def _rms(x, g):
    ms = jnp.mean(x * x, axis=-1, keepdims=True)
    return x * lax.rsqrt(ms + EPS) * g


def _silu(x):
    return x * (1.0 / (1.0 + jnp.exp(-x)))


def _gelu_tanh(x):
    c = 0.7978845608028654
    return x * (0.5 * (1.0 + jnp.tanh(c * (x + 0.044715 * (x * x * x)))))


def _dot(a, b):
    return jnp.dot(a, b, preferred_element_type=F32)


def _dot_nt(a, b, precision=None):
    return lax.dot_general(a, b, (((1,), (1,)), ((), ())),
                           precision=precision, preferred_element_type=F32)


def _cast_specs(casts, n_steps, step_of):
    in_specs, out_specs, out_shapes = [], [], []
    for w, layer in casts:
        _, rows, cols = w.shape
        n_blocks = max(nb for nb in range(1, n_steps + 1)
                       if rows % nb == 0 and (rows // nb) % BF16_ROWS == 0)

        def block_of(*g, n_blocks=n_blocks):
            return jnp.minimum(step_of(*g), n_blocks - 1)

        in_specs.append(pl.BlockSpec(
            (None, rows // n_blocks, cols),
            lambda *g, block_of=block_of, layer=layer: (layer, block_of(*g), 0)))
        out_specs.append(pl.BlockSpec(
            (rows // n_blocks, cols), lambda *g, block_of=block_of: (block_of(*g), 0)))
        out_shapes.append(jax.ShapeDtypeStruct((rows, cols), BF16))
    return in_specs, out_specs, out_shapes


def _cast_bytes(casts, n_steps):
    specs, _, _ = _cast_specs(casts, n_steps, lambda *g: 0)
    return sum(2 * (4 + 2) * s.block_shape[1] * s.block_shape[2] for s in specs)


def _run_casts(src_refs, dst_refs):
    for src, dst in zip(src_refs, dst_refs):
        dst[...] = src[...].astype(BF16)


def _ada_block(c_ref, w_ref, b_ref, o_ref):
    s = _silu(c_ref[...]).astype(BF16)
    o_ref[...] = _dot(s, w_ref[...].astype(BF16)) + b_ref[...]


def _ada_specs(ada_w, layer, tn, block_of):
    _, d, n_out = ada_w.shape
    in_specs = [
        pl.BlockSpec((MOD_ROWS, d), lambda *g: (0, 0)),
        pl.BlockSpec((None, d, tn), lambda *g: (layer, 0, block_of(*g))),
        pl.BlockSpec((None, 1, tn), lambda *g: (layer, 0, block_of(*g))),
    ]
    out_spec = pl.BlockSpec((MOD_ROWS, tn), lambda *g: (0, block_of(*g)))
    return in_specs, out_spec, jax.ShapeDtypeStruct((MOD_ROWS, n_out), F32)


def _ada_kernel(c_ref, w_ref, b_ref, *refs, n_cast):
    cast_src, o_ref, cast_dst = refs[:n_cast], refs[n_cast], refs[n_cast + 1:]
    _ada_block(c_ref, w_ref, b_ref, o_ref)
    _run_casts(cast_src, cast_dst)


def _ada_call(c_pad, ada_w, ada_b3, layer, casts):
    _, d, n_out = ada_w.shape
    n_cols = n_out // ADA_TN
    ada_in, ada_out, ada_shape = _ada_specs(ada_w, layer, ADA_TN, lambda n: n)
    cast_in, cast_out, cast_shapes = _cast_specs(casts, n_cols, lambda n: n)
    return pl.pallas_call(
        functools.partial(_ada_kernel, n_cast=len(casts)),
        out_shape=[ada_shape] + cast_shapes,
        grid=(n_cols,),
        in_specs=ada_in + cast_in,
        out_specs=[ada_out] + cast_out,
        compiler_params=pltpu.CompilerParams(
            dimension_semantics=("arbitrary",),
            vmem_limit_bytes=(2 * d * ADA_TN * (4 + 2) + _cast_bytes(casts, n_cols)
                              + VMEM_SLACK)),
        name="ada_mod",
    )(c_pad, ada_w, ada_b3, *[w for w, _ in casts])


def _ffn_kernel(x_ref, mod_ref, g_ref, w1_ref, w3_ref, w2_ref, *refs, sub, n_cast):
    cast_src, o_ref = refs[:n_cast], refs[n_cast]
    cast_dst, h_ref = refs[n_cast + 1:2 * n_cast + 1], refs[2 * n_cast + 1]
    _run_casts(cast_src, cast_dst)
    j = pl.program_id(1)
    tm, d_out = o_ref.shape

    @pl.when(j == 0)
    def _():
        shift = mod_ref[0, 3 * sub:3 * sub + 1, :]
        gain = g_ref[...] * (1.0 + mod_ref[0, 3 * sub + 1:3 * sub + 2, :])

        def norm_rows(r, carry):
            rows = pl.ds(pl.multiple_of(r * NORM_ROWS, NORM_ROWS), NORM_ROWS)
            x = x_ref[rows, :]
            o_ref[rows, :] = x
            inv = lax.rsqrt(jnp.mean(x * x, axis=-1, keepdims=True) + EPS)
            h_ref[rows, :] = (x * inv * gain + shift).astype(BF16)
            return carry

        lax.fori_loop(0, tm // NORM_ROWS, norm_rows, 0, unroll=NORM_UNROLL)

    gate = FFN_RES * mod_ref[0, 3 * sub + 2:3 * sub + 3, :]
    for r in range(tm // FFN_SUB_ROWS):
        rows = slice(r * FFN_SUB_ROWS, (r + 1) * FFN_SUB_ROWS)
        h = h_ref[rows, :]
        half = w1_ref.shape[1] // 2
        acts = []
        for p in range(2):
            ps = slice(p * half, (p + 1) * half)
            acts.append((_silu(_dot(h, w1_ref[:, ps])) * _dot(h, w3_ref[:, ps])).astype(BF16))
        for n in range(d_out // OUT_CHUNK):
            cs = slice(n * OUT_CHUNK, (n + 1) * OUT_CHUNK)
            o_ref[rows, cs] += gate[:, cs] * (_dot(acts[0], w2_ref[0:half, cs])
                                              + _dot(acts[1], w2_ref[half:2 * half, cs]))


def _ffn_call(x2, mod_l, g, w13, w2, sub, seq, casts=()):
    m, d = x2.shape
    d_ff = w2.shape[0]
    tiles_per_batch = seq // FFN_TM
    n_ff = d_ff // FFN_TF
    n_steps = (m // FFN_TM) * n_ff
    side_in, side_out, side_shapes = _cast_specs(casts, n_steps, lambda i, j: i * n_ff + j)
    vmem = (2 * 2 * FFN_TM * d * 4 + FFN_TM * d * 2 + 2 * 3 * d * FFN_TF * 2
            + FFN_SUB_ROWS * (3 * FFN_TF + OUT_CHUNK) * 4 + _cast_bytes(casts, n_steps)
            + VMEM_SLACK)
    outs = pl.pallas_call(
        functools.partial(_ffn_kernel, sub=sub, n_cast=len(casts)),
        out_shape=[jax.ShapeDtypeStruct((m, d), F32)] + side_shapes,
        grid=(m // FFN_TM, n_ff),
        in_specs=[
            pl.BlockSpec((FFN_TM, d), lambda i, j: (i, 0)),
            pl.BlockSpec((1, 9, d), lambda i, j: (i // tiles_per_batch, 0, 0)),
            pl.BlockSpec((1, d), lambda i, j: (0, 0)),
            pl.BlockSpec((d, FFN_TF), lambda i, j: (0, j)),
            pl.BlockSpec((d, FFN_TF), lambda i, j: (0, j + n_ff)),
            pl.BlockSpec((FFN_TF, d), lambda i, j: (j, 0)),
        ] + side_in,
        out_specs=[pl.BlockSpec((FFN_TM, d), lambda i, j: (i, 0))] + side_out,
        scratch_shapes=[pltpu.VMEM((FFN_TM, d), BF16)],
        compiler_params=pltpu.CompilerParams(
            dimension_semantics=("arbitrary", "arbitrary"),
            vmem_limit_bytes=vmem),
        name=f"ffn{sub}",
    )(x2, mod_l, g, w13, w13, w2, *[w for w, _ in casts])
    return outs[0], outs[1:]


def _in_mix_kernel(x_ref, mod_ref, g_ref, win_ref, poolw_ref, pools_ref, sguw_ref,
                   sgubt_ref, sgug_ref, gq_ref, gk_ref, convw_ref, ong_ref, *refs,
                   tiles_per_batch, n_cast):
    cast_src, refs = refs[:n_cast], refs[n_cast:]
    yabd_ref, q_ref, k_ref, vt_ref, kmean_ref = refs[:5]
    cast_dst, (pbuf, zbuf) = refs[5:5 + n_cast], refs[5 + n_cast:]
    _run_casts(cast_src, cast_dst)
    tm = x_ref.shape[0]
    tib = pl.program_id(0) % tiles_per_batch

    @pl.when(tib == 0)
    def _():
        pbuf[0:POOL_HALO, :] = jnp.zeros((POOL_HALO, GROUP_W), F32)
        zbuf[0:CONV_HALO, :] = jnp.zeros((CONV_HALO, GROUP_W), F32)

    shift = mod_ref[0, 3:4, :]
    scale = mod_ref[0, 4:5, :]
    h = (_rms(x_ref[...], g_ref[...]) * (1.0 + scale) + shift).astype(BF16)

    proj = {}

    def project(*groups):
        for g in groups:
            proj[g] = _dot(h, win_ref[:, g * GROUP_W:(g + 1) * GROUP_W])

    def heads():
        return [slice(hd * HEAD_DIM, (hd + 1) * HEAD_DIM) for hd in range(N_HEADS)]

    def pool_mixer():
        pbuf[POOL_HALO:POOL_HALO + tm, :] = proj[0]
        pos = tib * tm + lax.broadcasted_iota(jnp.int32, (tm, 1), 0)
        ya_parts = []
        for hd, (cs, win) in enumerate(zip(heads(), POOL_WINDOWS)):
            xg = pbuf[POOL_HALO:POOL_HALO + tm, cs]
            wsum = xg
            for lag in range(1, win):
                wsum = wsum + pbuf[POOL_HALO - lag:POOL_HALO - lag + tm, cs]
            cnt = jnp.minimum(pos + 1, win).astype(F32)
            dlt = wsum / cnt - xg
            ya_parts.append(_dot(dlt.astype(BF16), poolw_ref[hd].astype(BF16)))
        ya = jnp.concatenate(ya_parts, axis=1) * pools_ref[...]
        pbuf[0:POOL_HALO, :] = pbuf[tm:tm + POOL_HALO, :]
        yabd_ref[:, 0:GROUP_W] = _rms(ya, ong_ref[0:1, :]).astype(BF16)

    def sgu_mixer():
        u = _gelu_tanh(proj[1])
        v = _gelu_tanh(proj[2])
        r_t = lax.broadcasted_iota(jnp.int32, (SGU_CHUNK, SGU_CHUNK), 0)
        r_s = lax.broadcasted_iota(jnp.int32, (SGU_CHUNK, SGU_CHUNK), 1)
        yb_parts = []
        for hd, cs in enumerate(heads()):
            vh = _rms(v[:, cs], sgug_ref[:, cs]).astype(BF16)
            wm = jnp.where(r_t >= r_s, sguw_ref[hd], 0.0).astype(BF16)
            bias = sgubt_ref[:, hd:hd + 1]
            mixed = [_dot(wm, vh[c * SGU_CHUNK:(c + 1) * SGU_CHUNK, :]) + bias
                     for c in range(tm // SGU_CHUNK)]
            yb_parts.append(u[:, cs] * jnp.concatenate(mixed, axis=0))
        yb = jnp.concatenate(yb_parts, axis=1)
        yabd_ref[:, GROUP_W:2 * GROUP_W] = _rms(yb, ong_ref[1:2, :]).astype(BF16)

    def conv_mixer():
        zbuf[CONV_HALO:CONV_HALO + tm, :] = proj[7] * proj[8]
        conv = jnp.zeros((tm, GROUP_W), F32)
        for tap in range(CONV_WIDTH):
            off = CONV_HALO - (CONV_WIDTH - 1) + tap
            conv = conv + convw_ref[tap:tap + 1, :] * zbuf[off:off + tm, :]
        zbuf[0:CONV_HALO, :] = zbuf[tm:tm + CONV_HALO, :]
        yabd_ref[:, 2 * GROUP_W:3 * GROUP_W] = _rms(proj[6] * conv, ong_ref[3:4, :]).astype(BF16)

    def qk_prep():
        qk_scale = HEAD_DIM ** -0.5
        q = jnp.concatenate([_rms(proj[3][:, cs], gq_ref[...]) * qk_scale for cs in heads()],
                            axis=1)
        k = jnp.concatenate([_rms(proj[4][:, cs], gk_ref[...]) for cs in heads()], axis=1)
        q_ref[...] = q
        k_ref[...] = k.astype(BF16)
        kmean_ref[0] = jnp.mean(k, axis=0, keepdims=True)

    project(1, 2, 0, 3, 4)
    sgu_mixer()
    pool_mixer()
    qk_prep()
    project(6, 7, 8, 5)
    conv_mixer()
    vt_ref[0] = proj[5].T.astype(BF16)


def _in_mix_call(x2, mod_l, g, w_in, pool_w, pool_scale, sgu_w, sgu_bt, sgu_g,
                 gq, gk, conv_w, out_g, casts, seq):
    m, d = x2.shape
    tm = MOBA_BLOCK
    n_tiles = m // tm
    tiles_per_batch = seq // tm
    const2 = lambda i: (0, 0)
    const3 = lambda i: (0, 0, 0)
    cast_in, cast_out, cast_shapes = _cast_specs(casts, n_tiles, lambda i: i)
    vmem = (w_in.size * 2 + 2 * tm * d * 4 + 2 * tm * 6 * GROUP_W * 4
            + 2 * N_GROUPS_IN * tm * GROUP_W * 4 + _cast_bytes(casts, n_tiles) + VMEM_SLACK)
    outs = pl.pallas_call(
        functools.partial(_in_mix_kernel, tiles_per_batch=tiles_per_batch, n_cast=len(casts)),
        out_shape=[
            jax.ShapeDtypeStruct((m, 3 * GROUP_W), BF16),
            jax.ShapeDtypeStruct((m, GROUP_W), F32),
            jax.ShapeDtypeStruct((m, GROUP_W), BF16),
            jax.ShapeDtypeStruct((n_tiles, GROUP_W, tm), BF16),
            jax.ShapeDtypeStruct((n_tiles, 1, GROUP_W), F32),
        ] + cast_shapes,
        grid=(n_tiles,),
        in_specs=[
            pl.BlockSpec((tm, d), lambda i: (i, 0)),
            pl.BlockSpec((1, 9, d), lambda i: (i // tiles_per_batch, 0, 0)),
            pl.BlockSpec((1, d), const2),
            pl.BlockSpec(w_in.shape, const2),
            pl.BlockSpec(pool_w.shape, const3),
            pl.BlockSpec((1, GROUP_W), const2),
            pl.BlockSpec(sgu_w.shape, const3),
            pl.BlockSpec(sgu_bt.shape, const2),
            pl.BlockSpec((1, GROUP_W), const2),
            pl.BlockSpec((1, HEAD_DIM), const2),
            pl.BlockSpec((1, HEAD_DIM), const2),
            pl.BlockSpec(conv_w.shape, const2),
            pl.BlockSpec(out_g.shape, const2),
        ] + cast_in,
        out_specs=[
            pl.BlockSpec((tm, 3 * GROUP_W), lambda i: (i, 0)),
            pl.BlockSpec((tm, GROUP_W), lambda i: (i, 0)),
            pl.BlockSpec((tm, GROUP_W), lambda i: (i, 0)),
            pl.BlockSpec((1, GROUP_W, tm), lambda i: (i, 0, 0)),
            pl.BlockSpec((1, 1, GROUP_W), lambda i: (i, 0, 0)),
        ] + cast_out,
        scratch_shapes=[
            pltpu.VMEM((POOL_HALO + tm, GROUP_W), F32),
            pltpu.VMEM((CONV_HALO + tm, GROUP_W), F32),
        ],
        compiler_params=pltpu.CompilerParams(
            dimension_semantics=("arbitrary",),
            vmem_limit_bytes=vmem),
        name="in_mix",
    )(x2, mod_l, g, w_in, pool_w, pool_scale, sgu_w, sgu_bt, sgu_g, gq, gk, conv_w, out_g,
      *[w for w, _ in casts])
    return outs[:5], outs[5:]


def _moba_out_kernel(q_ref, k_ref, vt_ref, kmean_ref, yabd_ref, x_ref, mod_ref, ong_ref,
                     wout_ref, o_ref, qb_ref, ycn_ref, projl_ref, *, n_tiles):
    t = pl.program_id(0)
    tq = q_ref.shape[0]
    n_blk = kmean_ref.shape[1]
    own = jnp.minimum(t, n_tiles - 1) % n_blk
    neg_inf = -jnp.inf
    heads = [slice(hd * HEAD_DIM, (hd + 1) * HEAD_DIM) for hd in range(N_HEADS)]

    @pl.when(t == 0)
    def _():
        ycn_ref[...] = jnp.zeros_like(ycn_ref)
        projl_ref[...] = jnp.zeros_like(projl_ref)

    def tile(n_visit):
        blk_id = lax.broadcasted_iota(jnp.int32, (n_blk, tq), 0).astype(F32)
        past = blk_id < own.astype(F32)
        key_i = lax.broadcasted_iota(jnp.int32, (tq, tq), 0)
        qry_i = lax.broadcasted_iota(jnp.int32, (tq, tq), 1)

        n_chunks = o_ref.shape[1] // PROJ_CHUNK
        chunks_issued = [0]

        def issue_proj_chunks(units_done):
            while chunks_issued[0] * (n_visit + 1) < units_done * n_chunks:
                cs = slice(chunks_issued[0] * PROJ_CHUNK, (chunks_issued[0] + 1) * PROJ_CHUNK)
                projl_ref[:, cs] = (
                    _dot(yabd_ref[:, 0:2 * GROUP_W], wout_ref[0:2 * GROUP_W, cs])
                    + _dot(yabd_ref[:, 2 * GROUP_W:3 * GROUP_W],
                           wout_ref[3 * GROUP_W:4 * GROUP_W, cs]))
                chunks_issued[0] += 1

        qb_ref[...] = q_ref[...].astype(BF16)
        blk_scores = [_dot_nt(kmean_ref[0, :, cs], q_ref[:, cs], precision=lax.Precision.HIGHEST)
                      for cs in heads]
        own_scores = [_dot_nt(k_ref[0, own, :, cs], qb_ref[:, cs]) for cs in heads]

        def block_scores(nb):
            return [_dot_nt(k_ref[0, nb, :, cs], qb_ref[:, cs]) for cs in heads]

        scores_next = block_scores(0) if n_visit else None

        prev_proj = projl_ref[...] + _dot(ycn_ref[...], wout_ref[2 * GROUP_W:3 * GROUP_W, :])
        o_ref[...] = x_ref[...] + mod_ref[0, 5:6, :] * prev_proj

        issue_proj_chunks(1)
        sel, m, l, acc = [], [], [], []
        for hd, cs in enumerate(heads):
            sb = jnp.where(past, blk_scores[hd], neg_inf)
            chosen = jnp.zeros((n_blk, tq), F32)
            for _ in range(min(MOBA_TOPK, n_blk)):
                top = jnp.max(sb, axis=0, keepdims=True)
                idx = jnp.min(jnp.where(sb == top, blk_id, float(n_blk)), axis=0, keepdims=True)
                hit = blk_id == idx
                chosen = jnp.where(hit, 1.0, chosen)
                sb = jnp.where(hit, neg_inf, sb)
            sel.append(jnp.where(past, chosen, 0.0))

            s = jnp.where(key_i <= qry_i, own_scores[hd], neg_inf)
            m.append(jnp.max(s, axis=0, keepdims=True))
            p = jnp.exp(s - m[hd])
            l.append(jnp.sum(p, axis=0, keepdims=True))
            acc.append(_dot(vt_ref[0, own, cs, :], p.astype(BF16)))

        for nb in range(n_visit):
            scores = scores_next
            if nb + 1 < n_visit:
                scores_next = block_scores(nb + 1)
            issue_proj_chunks(nb + 2)
            for hd, cs in enumerate(heads):
                s = jnp.where(sel[hd][nb:nb + 1, :] > 0.0, scores[hd], neg_inf)
                m_new = jnp.maximum(m[hd], jnp.max(s, axis=0, keepdims=True))
                alpha = jnp.exp(m[hd] - m_new)
                p = jnp.exp(s - m_new)
                l[hd] = alpha * l[hd] + jnp.sum(p, axis=0, keepdims=True)
                acc[hd] = alpha * acc[hd] + _dot(vt_ref[0, nb, cs, :], p.astype(BF16))
                m[hd] = m_new

        yc = jnp.concatenate([(acc[hd] / l[hd]).T for hd in range(N_HEADS)], axis=1)
        ycn_ref[...] = _rms(yc, ong_ref[2:3, :]).astype(BF16)

    step = n_blk // MOBA_VARIANTS
    for v in range(MOBA_VARIANTS):
        in_range = jnp.logical_and(own >= v * step, own < (v + 1) * step)
        pl.when(in_range)(functools.partial(tile, (v + 1) * step - 1))


def _moba_out_call(q, k4, vt4, kmean, yabd, x2, mod_l, out_g, w_out):
    m, d = x2.shape
    n_batch, n_blk, tq, _ = k4.shape
    n_tiles = n_batch * n_blk
    cur = lambda t: jnp.minimum(t, n_tiles - 1)
    prev = lambda t: jnp.maximum(t - 1, 0)
    vmem = (2 * 2 * n_blk * tq * GROUP_W * 2 + w_out.size * 2 + 2 * 2 * tq * d * 4
            + 2 * tq * GROUP_W * (4 + 3 * 2) + 3 * tq * d * 4 + 6 * N_HEADS * tq * tq * 4
            + VMEM_SLACK)
    return pl.pallas_call(
        functools.partial(_moba_out_kernel, n_tiles=n_tiles),
        out_shape=jax.ShapeDtypeStruct((m, d), F32),
        grid=(n_tiles + 1,),
        in_specs=[
            pl.BlockSpec((tq, GROUP_W), lambda t: (cur(t), 0)),
            pl.BlockSpec((1, n_blk, tq, GROUP_W), lambda t: (cur(t) // n_blk, 0, 0, 0)),
            pl.BlockSpec((1, n_blk, GROUP_W, tq), lambda t: (cur(t) // n_blk, 0, 0, 0)),
            pl.BlockSpec((1, n_blk, GROUP_W), lambda t: (cur(t) // n_blk, 0, 0)),
            pl.BlockSpec((tq, 3 * GROUP_W), lambda t: (cur(t), 0)),
            pl.BlockSpec((tq, d), lambda t: (prev(t), 0)),
            pl.BlockSpec((1, 9, d), lambda t: (prev(t) // n_blk, 0, 0)),
            pl.BlockSpec(out_g.shape, lambda t: (0, 0)),
            pl.BlockSpec(w_out.shape, lambda t: (0, 0)),
        ],
        out_specs=pl.BlockSpec((tq, d), lambda t: (prev(t), 0)),
        scratch_shapes=[
            pltpu.VMEM((tq, GROUP_W), BF16),
            pltpu.VMEM((tq, GROUP_W), BF16),
            pltpu.VMEM((tq, d), F32),
        ],
        compiler_params=pltpu.CompilerParams(
            dimension_semantics=("arbitrary",),
            vmem_limit_bytes=vmem),
        name="moba_out",
    )(q, k4, vt4, kmean, yabd, x2, mod_l, out_g, w_out)


def kernel(x, c, ada_w, ada_b, norm_g, ffn1_w13, ffn1_w2, w_in, pool_w, pool_scale, sgu_w,
           sgu_b, sgu_norm_g, q_norm_g, k_norm_g, conv_w, out_norm_g, w_out, ffn2_w13, ffn2_w2):
    n_batch, seq, d = x.shape
    n_layers = ada_w.shape[0]
    assert seq % MOBA_BLOCK == 0 and seq % FFN_TM == 0 and MOBA_BLOCK % SGU_CHUNK == 0
    assert n_batch <= MOD_ROWS and w_in.shape[2] == N_GROUPS_IN * GROUP_W
    n_blk = seq // MOBA_BLOCK

    c_pad = jnp.zeros((MOD_ROWS, d), F32).at[:n_batch].set(c)
    ada_b3 = ada_b.reshape(n_layers, 1, ada_b.shape[1])
    mod0, w13_a, w2_a = _ada_call(c_pad, ada_w, ada_b3, 0, [(ffn1_w13, 0), (ffn1_w2, 0)])
    mods = [mod0] + [_ada_call(c_pad, ada_w, ada_b3, l, [])[0] for l in range(1, n_layers)]
    w_in_l = None

    x2 = x.reshape(n_batch * seq, d)
    for l in range(n_layers):
        mod_l = mods[l][:n_batch].reshape(n_batch, 9, d)
        x2, cast_weights = _ffn_call(x2, mod_l, norm_g[l, 0:1], w13_a, w2_a, 0, seq,
                                     [(w_in, l)] if w_in_l is None else [])
        if cast_weights:
            w_in_l, = cast_weights
        (yabd, q, k, vt, kmean), (w_out_l, w13_b, w2_b) = _in_mix_call(
            x2, mod_l, norm_g[l, 1:2], w_in_l, pool_w[l],
            pool_scale[l].reshape(1, GROUP_W), sgu_w[l], sgu_b[l].T,
            sgu_norm_g[l].reshape(1, GROUP_W), q_norm_g[l].reshape(1, HEAD_DIM),
            k_norm_g[l].reshape(1, HEAD_DIM), conv_w[l],
            out_norm_g[l].reshape(4, GROUP_W),
            [(w_out, l), (ffn2_w13, l), (ffn2_w2, l)], seq)
        x2 = _moba_out_call(
            q, k.reshape(n_batch, n_blk, MOBA_BLOCK, GROUP_W),
            vt.reshape(n_batch, n_blk, GROUP_W, MOBA_BLOCK),
            kmean.reshape(n_batch, n_blk, GROUP_W), yabd, x2, mod_l,
            out_norm_g[l].reshape(4, GROUP_W), w_out_l)
        next_casts = ([(ffn1_w13, l + 1), (ffn1_w2, l + 1), (w_in, l + 1)]
                      if l + 1 < n_layers else [])
        x2, cast_weights = _ffn_call(x2, mod_l, norm_g[l, 2:3], w13_b, w2_b, 2, seq, next_casts)
        if cast_weights:
            w13_a, w2_a, w_in_l = cast_weights
    return x2.reshape(n_batch, seq, d)
```

```python
import functools

import jax
import jax.numpy as jnp
from jax import lax
from jax.experimental import pallas as pl
from jax.experimental.pallas import tpu as pltpu

F32 = jnp.float32
BF16 = jnp.bfloat16

GROUP_W = 512
HEAD_DIM = 128
N_HEADS = GROUP_W // HEAD_DIM
N_GROUPS_IN = 9
POOL_WINDOWS = (2, 4, 8, 16)
POOL_HALO = 16
SGU_CHUNK = 128
MOBA_BLOCK = 256
MOBA_TOPK = 3
CONV_WIDTH = 3
CONV_HALO = 8
FFN_RES = 0.5
EPS = 1e-6
MOD_ROWS = 8
BF16_ROWS = 16

MIB = 1024 * 1024
VMEM_SLACK = 4 * MIB
ADA_TN = 512
FFN_TM = 1024
FFN_SUB_ROWS = 1024
FFN_TF = 512
OUT_CHUNK = 512
MOBA_VARIANTS = 8
PROJ_CHUNK = 256
NORM_ROWS = 16
NORM_UNROLL = 8


def _rms(x, g):
    ms = jnp.mean(x * x, axis=-1, keepdims=True)
    return x * lax.rsqrt(ms + EPS) * g


def _silu(x):
    return x * (1.0 / (1.0 + jnp.exp(-x)))


def _gelu_tanh(x):
    c = 0.7978845608028654
    return x * (0.5 * (1.0 + jnp.tanh(c * (x + 0.044715 * (x * x * x)))))


def _dot(a, b):
    return jnp.dot(a, b, preferred_element_type=F32)


def _dot_nt(a, b, precision=None):
    return lax.dot_general(a, b, (((1,), (1,)), ((), ())),
                           precision=precision, preferred_element_type=F32)


def _cast_specs(casts, n_steps, step_of):
    in_specs, out_specs, out_shapes = [], [], []
    for w, layer in casts:
        _, rows, cols = w.shape
        n_blocks = max(nb for nb in range(1, n_steps + 1)
                       if rows % nb == 0 and (rows // nb) % BF16_ROWS == 0)

        def block_of(*g, n_blocks=n_blocks):
            return jnp.minimum(step_of(*g), n_blocks - 1)

        in_specs.append(pl.BlockSpec(
            (None, rows // n_blocks, cols),
            lambda *g, block_of=block_of, layer=layer: (layer, block_of(*g), 0)))
        out_specs.append(pl.BlockSpec(
            (rows // n_blocks, cols), lambda *g, block_of=block_of: (block_of(*g), 0)))
        out_shapes.append(jax.ShapeDtypeStruct((rows, cols), BF16))
    return in_specs, out_specs, out_shapes


def _cast_bytes(casts, n_steps):
    specs, _, _ = _cast_specs(casts, n_steps, lambda *g: 0)
    return sum(2 * (4 + 2) * s.block_shape[1] * s.block_shape[2] for s in specs)


def _run_casts(src_refs, dst_refs):
    for src, dst in zip(src_refs, dst_refs):
        dst[...] = src[...].astype(BF16)


def _ada_block(c_ref, w_ref, b_ref, o_ref):
    s = _silu(c_ref[...]).astype(BF16)
    o_ref[...] = _dot(s, w_ref[...].astype(BF16)) + b_ref[...]


def _ada_specs(ada_w, layer, tn, block_of):
    _, d, n_out = ada_w.shape
    in_specs = [
        pl.BlockSpec((MOD_ROWS, d), lambda *g: (0, 0)),
        pl.BlockSpec((None, d, tn), lambda *g: (layer, 0, block_of(*g))),
        pl.BlockSpec((None, 1, tn), lambda *g: (layer, 0, block_of(*g))),
    ]
    out_spec = pl.BlockSpec((MOD_ROWS, tn), lambda *g: (0, block_of(*g)))
    return in_specs, out_spec, jax.ShapeDtypeStruct((MOD_ROWS, n_out), F32)


def _ada_kernel(c_ref, w_ref, b_ref, *refs, n_cast):
    cast_src, o_ref, cast_dst = refs[:n_cast], refs[n_cast], refs[n_cast + 1:]
    _ada_block(c_ref, w_ref, b_ref, o_ref)
    _run_casts(cast_src, cast_dst)


def _ada_call(c_pad, ada_w, ada_b3, layer, casts):
    _, d, n_out = ada_w.shape
    n_cols = n_out // ADA_TN
    ada_in, ada_out, ada_shape = _ada_specs(ada_w, layer, ADA_TN, lambda n: n)
    cast_in, cast_out, cast_shapes = _cast_specs(casts, n_cols, lambda n: n)
    return pl.pallas_call(
        functools.partial(_ada_kernel, n_cast=len(casts)),
        out_shape=[ada_shape] + cast_shapes,
        grid=(n_cols,),
        in_specs=ada_in + cast_in,
        out_specs=[ada_out] + cast_out,
        compiler_params=pltpu.CompilerParams(
            dimension_semantics=("arbitrary",),
            vmem_limit_bytes=(2 * d * ADA_TN * (4 + 2) + _cast_bytes(casts, n_cols)
                              + VMEM_SLACK)),
        name="ada_mod",
    )(c_pad, ada_w, ada_b3, *[w for w, _ in casts])


def _ffn_kernel(x_ref, mod_ref, g_ref, w1_ref, w3_ref, w2_ref, *refs, sub, n_cast):
    cast_src, o_ref = refs[:n_cast], refs[n_cast]
    cast_dst, h_ref = refs[n_cast + 1:2 * n_cast + 1], refs[2 * n_cast + 1]
    _run_casts(cast_src, cast_dst)
    j = pl.program_id(1)
    tm, d_out = o_ref.shape

    @pl.when(j == 0)
    def _():
        shift = mod_ref[0, 3 * sub:3 * sub + 1, :]
        gain = g_ref[...] * (1.0 + mod_ref[0, 3 * sub + 1:3 * sub + 2, :])

        def norm_rows(r, carry):
            rows = pl.ds(pl.multiple_of(r * NORM_ROWS, NORM_ROWS), NORM_ROWS)
            x = x_ref[rows, :]
            o_ref[rows, :] = x
            inv = lax.rsqrt(jnp.mean(x * x, axis=-1, keepdims=True) + EPS)
            h_ref[rows, :] = (x * inv * gain + shift).astype(BF16)
            return carry

        lax.fori_loop(0, tm // NORM_ROWS, norm_rows, 0, unroll=NORM_UNROLL)

    gate = FFN_RES * mod_ref[0, 3 * sub + 2:3 * sub + 3, :]
    for r in range(tm // FFN_SUB_ROWS):
        rows = slice(r * FFN_SUB_ROWS, (r + 1) * FFN_SUB_ROWS)
        h = h_ref[rows, :]
        half = w1_ref.shape[1] // 2
        acts = []
        for p in range(2):
            ps = slice(p * half, (p + 1) * half)
            acts.append((_silu(_dot(h, w1_ref[:, ps])) * _dot(h, w3_ref[:, ps])).astype(BF16))
        for n in range(d_out // OUT_CHUNK):
            cs = slice(n * OUT_CHUNK, (n + 1) * OUT_CHUNK)
            o_ref[rows, cs] += gate[:, cs] * (_dot(acts[0], w2_ref[0:half, cs])
                                              + _dot(acts[1], w2_ref[half:2 * half, cs]))


def _ffn_call(x2, mod_l, g, w13, w2, sub, seq, casts=()):
    m, d = x2.shape
    d_ff = w2.shape[0]
    tiles_per_batch = seq // FFN_TM
    n_ff = d_ff // FFN_TF
    n_steps = (m // FFN_TM) * n_ff
    side_in, side_out, side_shapes = _cast_specs(casts, n_steps, lambda i, j: i * n_ff + j)
    vmem = (2 * 2 * FFN_TM * d * 4 + FFN_TM * d * 2 + 2 * 3 * d * FFN_TF * 2
            + FFN_SUB_ROWS * (3 * FFN_TF + OUT_CHUNK) * 4 + _cast_bytes(casts, n_steps)
            + VMEM_SLACK)
    outs = pl.pallas_call(
        functools.partial(_ffn_kernel, sub=sub, n_cast=len(casts)),
        out_shape=[jax.ShapeDtypeStruct((m, d), F32)] + side_shapes,
        grid=(m // FFN_TM, n_ff),
        in_specs=[
            pl.BlockSpec((FFN_TM, d), lambda i, j: (i, 0)),
            pl.BlockSpec((1, 9, d), lambda i, j: (i // tiles_per_batch, 0, 0)),
            pl.BlockSpec((1, d), lambda i, j: (0, 0)),
            pl.BlockSpec((d, FFN_TF), lambda i, j: (0, j)),
            pl.BlockSpec((d, FFN_TF), lambda i, j: (0, j + n_ff)),
            pl.BlockSpec((FFN_TF, d), lambda i, j: (j, 0)),
        ] + side_in,
        out_specs=[pl.BlockSpec((FFN_TM, d), lambda i, j: (i, 0))] + side_out,
        scratch_shapes=[pltpu.VMEM((FFN_TM, d), BF16)],
        compiler_params=pltpu.CompilerParams(
            dimension_semantics=("arbitrary", "arbitrary"),
            vmem_limit_bytes=vmem),
        name=f"ffn{sub}",
    )(x2, mod_l, g, w13, w13, w2, *[w for w, _ in casts])
    return outs[0], outs[1:]


def _in_mix_kernel(x_ref, mod_ref, g_ref, win_ref, poolw_ref, pools_ref, sguw_ref,
                   sgubt_ref, sgug_ref, gq_ref, gk_ref, convw_ref, ong_ref, *refs,
                   tiles_per_batch, n_cast):
    cast_src, refs = refs[:n_cast], refs[n_cast:]
    yabd_ref, q_ref, k_ref, vt_ref, kmean_ref = refs[:5]
    cast_dst, (pbuf, zbuf) = refs[5:5 + n_cast], refs[5 + n_cast:]
    _run_casts(cast_src, cast_dst)
    tm = x_ref.shape[0]
    tib = pl.program_id(0) % tiles_per_batch

    @pl.when(tib == 0)
    def _():
        pbuf[0:POOL_HALO, :] = jnp.zeros((POOL_HALO, GROUP_W), F32)
        zbuf[0:CONV_HALO, :] = jnp.zeros((CONV_HALO, GROUP_W), F32)

    shift = mod_ref[0, 3:4, :]
    scale = mod_ref[0, 4:5, :]
    h = (_rms(x_ref[...], g_ref[...]) * (1.0 + scale) + shift).astype(BF16)

    proj = {}

    def project(*groups):
        for g in groups:
            proj[g] = _dot(h, win_ref[:, g * GROUP_W:(g + 1) * GROUP_W])

    def heads():
        return [slice(hd * HEAD_DIM, (hd + 1) * HEAD_DIM) for hd in range(N_HEADS)]

    def pool_mixer():
        pbuf[POOL_HALO:POOL_HALO + tm, :] = proj[0]
        pos = tib * tm + lax.broadcasted_iota(jnp.int32, (tm, 1), 0)
        ya_parts = []
        for hd, (cs, win) in enumerate(zip(heads(), POOL_WINDOWS)):
            xg = pbuf[POOL_HALO:POOL_HALO + tm, cs]
            wsum = xg
            for lag in range(1, win):
                wsum = wsum + pbuf[POOL_HALO - lag:POOL_HALO - lag + tm, cs]
            cnt = jnp.minimum(pos + 1, win).astype(F32)
            dlt = wsum / cnt - xg
            ya_parts.append(_dot(dlt.astype(BF16), poolw_ref[hd].astype(BF16)))
        ya = jnp.concatenate(ya_parts, axis=1) * pools_ref[...]
        pbuf[0:POOL_HALO, :] = pbuf[tm:tm + POOL_HALO, :]
        yabd_ref[:, 0:GROUP_W] = _rms(ya, ong_ref[0:1, :]).astype(BF16)

    def sgu_mixer():
        u = _gelu_tanh(proj[1])
        v = _gelu_tanh(proj[2])
        r_t = lax.broadcasted_iota(jnp.int32, (SGU_CHUNK, SGU_CHUNK), 0)
        r_s = lax.broadcasted_iota(jnp.int32, (SGU_CHUNK, SGU_CHUNK), 1)
        yb_parts = []
        for hd, cs in enumerate(heads()):
            vh = _rms(v[:, cs], sgug_ref[:, cs]).astype(BF16)
            wm = jnp.where(r_t >= r_s, sguw_ref[hd], 0.0).astype(BF16)
            bias = sgubt_ref[:, hd:hd + 1]
            mixed = [_dot(wm, vh[c * SGU_CHUNK:(c + 1) * SGU_CHUNK, :]) + bias
                     for c in range(tm // SGU_CHUNK)]
            yb_parts.append(u[:, cs] * jnp.concatenate(mixed, axis=0))
        yb = jnp.concatenate(yb_parts, axis=1)
        yabd_ref[:, GROUP_W:2 * GROUP_W] = _rms(yb, ong_ref[1:2, :]).astype(BF16)

    def conv_mixer():
        zbuf[CONV_HALO:CONV_HALO + tm, :] = proj[7] * proj[8]
        conv = jnp.zeros((tm, GROUP_W), F32)
        for tap in range(CONV_WIDTH):
            off = CONV_HALO - (CONV_WIDTH - 1) + tap
            conv = conv + convw_ref[tap:tap + 1, :] * zbuf[off:off + tm, :]
        zbuf[0:CONV_HALO, :] = zbuf[tm:tm + CONV_HALO, :]
        yabd_ref[:, 2 * GROUP_W:3 * GROUP_W] = _rms(proj[6] * conv, ong_ref[3:4, :]).astype(BF16)

    def qk_prep():
        qk_scale = HEAD_DIM ** -0.5
        q = jnp.concatenate([_rms(proj[3][:, cs], gq_ref[...]) * qk_scale for cs in heads()],
                            axis=1)
        k = jnp.concatenate([_rms(proj[4][:, cs], gk_ref[...]) for cs in heads()], axis=1)
        q_ref[...] = q
        k_ref[...] = k.astype(BF16)
        kmean_ref[0] = jnp.mean(k, axis=0, keepdims=True)

    project(1, 2, 0, 3, 4)
    sgu_mixer()
    pool_mixer()
    qk_prep()
    project(6, 7, 8, 5)
    conv_mixer()
    vt_ref[0] = proj[5].T.astype(BF16)


def _in_mix_call(x2, mod_l, g, w_in, pool_w, pool_scale, sgu_w, sgu_bt, sgu_g,
                 gq, gk, conv_w, out_g, casts, seq):
    m, d = x2.shape
    tm = MOBA_BLOCK
    n_tiles = m // tm
    tiles_per_batch = seq // tm
    const2 = lambda i: (0, 0)
    const3 = lambda i: (0, 0, 0)
    cast_in, cast_out, cast_shapes = _cast_specs(casts, n_tiles, lambda i: i)
    vmem = (w_in.size * 2 + 2 * tm * d * 4 + 2 * tm * 6 * GROUP_W * 4
            + 2 * N_GROUPS_IN * tm * GROUP_W * 4 + _cast_bytes(casts, n_tiles) + VMEM_SLACK)
    outs = pl.pallas_call(
        functools.partial(_in_mix_kernel, tiles_per_batch=tiles_per_batch, n_cast=len(casts)),
        out_shape=[
            jax.ShapeDtypeStruct((m, 3 * GROUP_W), BF16),
            jax.ShapeDtypeStruct((m, GROUP_W), F32),
            jax.ShapeDtypeStruct((m, GROUP_W), BF16),
            jax.ShapeDtypeStruct((n_tiles, GROUP_W, tm), BF16),
            jax.ShapeDtypeStruct((n_tiles, 1, GROUP_W), F32),
        ] + cast_shapes,
        grid=(n_tiles,),
        in_specs=[
            pl.BlockSpec((tm, d), lambda i: (i, 0)),
            pl.BlockSpec((1, 9, d), lambda i: (i // tiles_per_batch, 0, 0)),
            pl.BlockSpec((1, d), const2),
            pl.BlockSpec(w_in.shape, const2),
            pl.BlockSpec(pool_w.shape, const3),
            pl.BlockSpec((1, GROUP_W), const2),
            pl.BlockSpec(sgu_w.shape, const3),
            pl.BlockSpec(sgu_bt.shape, const2),
            pl.BlockSpec((1, GROUP_W), const2),
            pl.BlockSpec((1, HEAD_DIM), const2),
            pl.BlockSpec((1, HEAD_DIM), const2),
            pl.BlockSpec(conv_w.shape, const2),
            pl.BlockSpec(out_g.shape, const2),
        ] + cast_in,
        out_specs=[
            pl.BlockSpec((tm, 3 * GROUP_W), lambda i: (i, 0)),
            pl.BlockSpec((tm, GROUP_W), lambda i: (i, 0)),
            pl.BlockSpec((tm, GROUP_W), lambda i: (i, 0)),
            pl.BlockSpec((1, GROUP_W, tm), lambda i: (i, 0, 0)),
            pl.BlockSpec((1, 1, GROUP_W), lambda i: (i, 0, 0)),
        ] + cast_out,
        scratch_shapes=[
            pltpu.VMEM((POOL_HALO + tm, GROUP_W), F32),
            pltpu.VMEM((CONV_HALO + tm, GROUP_W), F32),
        ],
        compiler_params=pltpu.CompilerParams(
            dimension_semantics=("arbitrary",),
            vmem_limit_bytes=vmem),
        name="in_mix",
    )(x2, mod_l, g, w_in, pool_w, pool_scale, sgu_w, sgu_bt, sgu_g, gq, gk, conv_w, out_g,
      *[w for w, _ in casts])
    return outs[:5], outs[5:]


def _moba_out_kernel(q_ref, k_ref, vt_ref, kmean_ref, yabd_ref, x_ref, mod_ref, ong_ref,
                     wout_ref, o_ref, qb_ref, ycn_ref, projl_ref, *, n_tiles):
    t = pl.program_id(0)
    tq = q_ref.shape[0]
    n_blk = kmean_ref.shape[1]
    own = jnp.minimum(t, n_tiles - 1) % n_blk
    neg_inf = -jnp.inf
    heads = [slice(hd * HEAD_DIM, (hd + 1) * HEAD_DIM) for hd in range(N_HEADS)]

    @pl.when(t == 0)
    def _():
        ycn_ref[...] = jnp.zeros_like(ycn_ref)
        projl_ref[...] = jnp.zeros_like(projl_ref)

    def finish_previous_tile():
        prev_proj = projl_ref[...] + _dot(ycn_ref[...], wout_ref[2 * GROUP_W:3 * GROUP_W, :])
        o_ref[...] = x_ref[...] + mod_ref[0, 5:6, :] * prev_proj

    def tile(n_visit):
        blk_id = lax.broadcasted_iota(jnp.int32, (n_blk, tq), 0).astype(F32)
        past = blk_id < own.astype(F32)
        key_i = lax.broadcasted_iota(jnp.int32, (tq, tq), 0)
        qry_i = lax.broadcasted_iota(jnp.int32, (tq, tq), 1)

        n_chunks = o_ref.shape[1] // PROJ_CHUNK
        chunks_issued = [0]

        def issue_proj_chunks(units_done):
            while chunks_issued[0] * (n_visit + 1) < units_done * n_chunks:
                cs = slice(chunks_issued[0] * PROJ_CHUNK, (chunks_issued[0] + 1) * PROJ_CHUNK)
                projl_ref[:, cs] = (
                    _dot(yabd_ref[:, 0:2 * GROUP_W], wout_ref[0:2 * GROUP_W, cs])
                    + _dot(yabd_ref[:, 2 * GROUP_W:3 * GROUP_W],
                           wout_ref[3 * GROUP_W:4 * GROUP_W, cs]))
                chunks_issued[0] += 1

        qb_ref[...] = q_ref[...].astype(BF16)
        blk_scores = [_dot_nt(kmean_ref[0, :, cs], q_ref[:, cs], precision=lax.Precision.HIGHEST)
                      for cs in heads]
        own_scores = [_dot_nt(k_ref[0, own, :, cs], qb_ref[:, cs]) for cs in heads]

        def block_scores(nb):
            return [_dot_nt(k_ref[0, nb, :, cs], qb_ref[:, cs]) for cs in heads]

        scores_next = block_scores(0) if n_visit else None

        finish_previous_tile()

        issue_proj_chunks(1)
        sel, m, l, acc = [], [], [], []
        for hd, cs in enumerate(heads):
            sb = jnp.where(past, blk_scores[hd], neg_inf)
            chosen = jnp.zeros((n_blk, tq), F32)
            for _ in range(min(MOBA_TOPK, n_blk)):
                top = jnp.max(sb, axis=0, keepdims=True)
                idx = jnp.min(jnp.where(sb == top, blk_id, float(n_blk)), axis=0, keepdims=True)
                hit = blk_id == idx
                chosen = jnp.where(hit, 1.0, chosen)
                sb = jnp.where(hit, neg_inf, sb)
            sel.append(jnp.where(past, chosen, 0.0))

            s = jnp.where(key_i <= qry_i, own_scores[hd], neg_inf)
            m.append(jnp.max(s, axis=0, keepdims=True))
            p = jnp.exp(s - m[hd])
            l.append(jnp.sum(p, axis=0, keepdims=True))
            acc.append(_dot(vt_ref[0, own, cs, :], p.astype(BF16)))

        for nb in range(n_visit):
            scores = scores_next
            if nb + 1 < n_visit:
                scores_next = block_scores(nb + 1)
            issue_proj_chunks(nb + 2)
            for hd, cs in enumerate(heads):
                s = jnp.where(sel[hd][nb:nb + 1, :] > 0.0, scores[hd], neg_inf)
                m_new = jnp.maximum(m[hd], jnp.max(s, axis=0, keepdims=True))
                alpha = jnp.exp(m[hd] - m_new)
                p = jnp.exp(s - m_new)
                l[hd] = alpha * l[hd] + jnp.sum(p, axis=0, keepdims=True)
                acc[hd] = alpha * acc[hd] + _dot(vt_ref[0, nb, cs, :], p.astype(BF16))
                m[hd] = m_new

        yc = jnp.concatenate([(acc[hd] / l[hd]).T for hd in range(N_HEADS)], axis=1)
        ycn_ref[...] = _rms(yc, ong_ref[2:3, :]).astype(BF16)

    step = n_blk // MOBA_VARIANTS
    for v in range(MOBA_VARIANTS):
        in_range = (t < n_tiles) & (own >= v * step) & (own < (v + 1) * step)
        pl.when(in_range)(functools.partial(tile, (v + 1) * step - 1))
    pl.when(t == n_tiles)(finish_previous_tile)


def _moba_out_call(q, k4, vt4, kmean, yabd, x2, mod_l, out_g, w_out):
    m, d = x2.shape
    n_batch, n_blk, tq, _ = k4.shape
    n_tiles = n_batch * n_blk
    cur = lambda t: jnp.minimum(t, n_tiles - 1)
    prev = lambda t: jnp.maximum(t - 1, 0)
    vmem = (2 * 2 * n_blk * tq * GROUP_W * 2 + w_out.size * 2 + 2 * 2 * tq * d * 4
            + 2 * tq * GROUP_W * (4 + 3 * 2) + 3 * tq * d * 4 + 6 * N_HEADS * tq * tq * 4
            + VMEM_SLACK)
    return pl.pallas_call(
        functools.partial(_moba_out_kernel, n_tiles=n_tiles),
        out_shape=jax.ShapeDtypeStruct((m, d), F32),
        grid=(n_tiles + 1,),
        in_specs=[
            pl.BlockSpec((tq, GROUP_W), lambda t: (cur(t), 0)),
            pl.BlockSpec((1, n_blk, tq, GROUP_W), lambda t: (cur(t) // n_blk, 0, 0, 0)),
            pl.BlockSpec((1, n_blk, GROUP_W, tq), lambda t: (cur(t) // n_blk, 0, 0, 0)),
            pl.BlockSpec((1, n_blk, GROUP_W), lambda t: (cur(t) // n_blk, 0, 0)),
            pl.BlockSpec((tq, 3 * GROUP_W), lambda t: (cur(t), 0)),
            pl.BlockSpec((tq, d), lambda t: (prev(t), 0)),
            pl.BlockSpec((1, 9, d), lambda t: (prev(t) // n_blk, 0, 0)),
            pl.BlockSpec(out_g.shape, lambda t: (0, 0)),
            pl.BlockSpec(w_out.shape, lambda t: (0, 0)),
        ],
        out_specs=pl.BlockSpec((tq, d), lambda t: (prev(t), 0)),
        scratch_shapes=[
            pltpu.VMEM((tq, GROUP_W), BF16),
            pltpu.VMEM((tq, GROUP_W), BF16),
            pltpu.VMEM((tq, d), F32),
        ],
        compiler_params=pltpu.CompilerParams(
            dimension_semantics=("arbitrary",),
            vmem_limit_bytes=vmem),
        name="moba_out",
    )(q, k4, vt4, kmean, yabd, x2, mod_l, out_g, w_out)


def kernel(x, c, ada_w, ada_b, norm_g, ffn1_w13, ffn1_w2, w_in, pool_w, pool_scale, sgu_w,
           sgu_b, sgu_norm_g, q_norm_g, k_norm_g, conv_w, out_norm_g, w_out, ffn2_w13, ffn2_w2):
    n_batch, seq, d = x.shape
    n_layers = ada_w.shape[0]
    assert seq % MOBA_BLOCK == 0 and seq % FFN_TM == 0 and MOBA_BLOCK % SGU_CHUNK == 0
    assert n_batch <= MOD_ROWS and w_in.shape[2] == N_GROUPS_IN * GROUP_W
    n_blk = seq // MOBA_BLOCK

    c_pad = jnp.zeros((MOD_ROWS, d), F32).at[:n_batch].set(c)
    ada_b3 = ada_b.reshape(n_layers, 1, ada_b.shape[1])
    mod0, w13_a, w2_a = _ada_call(c_pad, ada_w, ada_b3, 0, [(ffn1_w13, 0), (ffn1_w2, 0)])
    mods = [mod0] + [_ada_call(c_pad, ada_w, ada_b3, l, [])[0] for l in range(1, n_layers)]
    w_in_l = None

    x2 = x.reshape(n_batch * seq, d)
    for l in range(n_layers):
        mod_l = mods[l][:n_batch].reshape(n_batch, 9, d)
        x2, cast_weights = _ffn_call(x2, mod_l, norm_g[l, 0:1], w13_a, w2_a, 0, seq,
                                     [(w_in, l)] if w_in_l is None else [])
        if cast_weights:
            w_in_l, = cast_weights
        (yabd, q, k, vt, kmean), (w_out_l, w13_b, w2_b) = _in_mix_call(
            x2, mod_l, norm_g[l, 1:2], w_in_l, pool_w[l],
            pool_scale[l].reshape(1, GROUP_W), sgu_w[l], sgu_b[l].T,
            sgu_norm_g[l].reshape(1, GROUP_W), q_norm_g[l].reshape(1, HEAD_DIM),
            k_norm_g[l].reshape(1, HEAD_DIM), conv_w[l],
            out_norm_g[l].reshape(4, GROUP_W),
            [(w_out, l), (ffn2_w13, l), (ffn2_w2, l)], seq)
        x2 = _moba_out_call(
            q, k.reshape(n_batch, n_blk, MOBA_BLOCK, GROUP_W),
            vt.reshape(n_batch, n_blk, GROUP_W, MOBA_BLOCK),
            kmean.reshape(n_batch, n_blk, GROUP_W), yabd, x2, mod_l,
            out_norm_g[l].reshape(4, GROUP_W), w_out_l)
        next_casts = ([(ffn1_w13, l + 1), (ffn1_w2, l + 1), (w_in, l + 1)]
                      if l + 1 < n_layers else [])
        x2, cast_weights = _ffn_call(x2, mod_l, norm_g[l, 2:3], w13_b, w2_b, 2, seq, next_casts)
        if cast_weights:
            w13_a, w2_a, w_in_l = cast_weights
    return x2.reshape(n_batch, seq, d)
```

```python
import functools

import jax
import jax.numpy as jnp
from jax import lax
from jax.experimental import pallas as pl
from jax.experimental.pallas import tpu as pltpu

F32 = jnp.float32
BF16 = jnp.bfloat16

GROUP_W = 512
HEAD_DIM = 128
N_HEADS = GROUP_W // HEAD_DIM
N_GROUPS_IN = 9
POOL_WINDOWS = (2, 4, 8, 16)
POOL_HALO = 16
SGU_CHUNK = 128
MOBA_BLOCK = 256
MOBA_TOPK = 3
CONV_WIDTH = 3
CONV_HALO = 8
FFN_RES = 0.5
EPS = 1e-6
MOD_ROWS = 8
BF16_ROWS = 16

MIB = 1024 * 1024
VMEM_SLACK = 4 * MIB
ADA_TN = 512
FFN_TM = 1024
FFN_SUB_ROWS = 1024
FFN_TF = 512
OUT_CHUNK = 512
MOBA_VARIANTS = 2
PROJ_CHUNK = 256
NORM_ROWS = 16
NORM_UNROLL = 8


def _rms(x, g):
    ms = jnp.mean(x * x, axis=-1, keepdims=True)
    return x * lax.rsqrt(ms + EPS) * g


def _silu(x):
    return x * (1.0 / (1.0 + jnp.exp(-x)))


def _gelu_tanh(x):
    c = 0.7978845608028654
    return x * (0.5 * (1.0 + jnp.tanh(c * (x + 0.044715 * (x * x * x)))))


def _dot(a, b):
    return jnp.dot(a, b, preferred_element_type=F32)


def _dot_nt(a, b, precision=None):
    return lax.dot_general(a, b, (((1,), (1,)), ((), ())),
                           precision=precision, preferred_element_type=F32)


def _cast_specs(casts, n_steps, step_of):
    in_specs, out_specs, out_shapes = [], [], []
    for w, layer in casts:
        _, rows, cols = w.shape
        n_blocks = max(nb for nb in range(1, n_steps + 1)
                       if rows % nb == 0 and (rows // nb) % BF16_ROWS == 0)

        def block_of(*g, n_blocks=n_blocks):
            return jnp.minimum(step_of(*g), n_blocks - 1)

        in_specs.append(pl.BlockSpec(
            (None, rows // n_blocks, cols),
            lambda *g, block_of=block_of, layer=layer: (layer, block_of(*g), 0)))
        out_specs.append(pl.BlockSpec(
            (rows // n_blocks, cols), lambda *g, block_of=block_of: (block_of(*g), 0)))
        out_shapes.append(jax.ShapeDtypeStruct((rows, cols), BF16))
    return in_specs, out_specs, out_shapes


def _cast_bytes(casts, n_steps):
    specs, _, _ = _cast_specs(casts, n_steps, lambda *g: 0)
    return sum(2 * (4 + 2) * s.block_shape[1] * s.block_shape[2] for s in specs)


def _run_casts(src_refs, dst_refs):
    for src, dst in zip(src_refs, dst_refs):
        dst[...] = src[...].astype(BF16)


def _ada_block(c_ref, w_ref, b_ref, o_ref):
    s = _silu(c_ref[...]).astype(BF16)
    o_ref[...] = _dot(s, w_ref[...].astype(BF16)) + b_ref[...]


def _ada_specs(ada_w, layer, tn, block_of):
    _, d, n_out = ada_w.shape
    in_specs = [
        pl.BlockSpec((MOD_ROWS, d), lambda *g: (0, 0)),
        pl.BlockSpec((None, d, tn), lambda *g: (layer, 0, block_of(*g))),
        pl.BlockSpec((None, 1, tn), lambda *g: (layer, 0, block_of(*g))),
    ]
    out_spec = pl.BlockSpec((MOD_ROWS, tn), lambda *g: (0, block_of(*g)))
    return in_specs, out_spec, jax.ShapeDtypeStruct((MOD_ROWS, n_out), F32)


def _ada_kernel(c_ref, w_ref, b_ref, *refs, n_cast):
    cast_src, o_ref, cast_dst = refs[:n_cast], refs[n_cast], refs[n_cast + 1:]
    _ada_block(c_ref, w_ref, b_ref, o_ref)
    _run_casts(cast_src, cast_dst)


def _ada_call(c_pad, ada_w, ada_b3, layer, casts):
    _, d, n_out = ada_w.shape
    n_cols = n_out // ADA_TN
    ada_in, ada_out, ada_shape = _ada_specs(ada_w, layer, ADA_TN, lambda n: n)
    cast_in, cast_out, cast_shapes = _cast_specs(casts, n_cols, lambda n: n)
    return pl.pallas_call(
        functools.partial(_ada_kernel, n_cast=len(casts)),
        out_shape=[ada_shape] + cast_shapes,
        grid=(n_cols,),
        in_specs=ada_in + cast_in,
        out_specs=[ada_out] + cast_out,
        compiler_params=pltpu.CompilerParams(
            dimension_semantics=("arbitrary",),
            vmem_limit_bytes=(2 * d * ADA_TN * (4 + 2) + _cast_bytes(casts, n_cols)
                              + VMEM_SLACK)),
        name="ada_mod",
    )(c_pad, ada_w, ada_b3, *[w for w, _ in casts])


def _ffn_kernel(x_ref, mod_ref, g_ref, w1_ref, w3_ref, w2_ref, *refs, sub, n_cast):
    cast_src, o_ref = refs[:n_cast], refs[n_cast]
    cast_dst, h_ref = refs[n_cast + 1:2 * n_cast + 1], refs[2 * n_cast + 1]
    _run_casts(cast_src, cast_dst)
    j = pl.program_id(1)
    tm, d_out = o_ref.shape

    @pl.when(j == 0)
    def _():
        shift = mod_ref[0, 3 * sub:3 * sub + 1, :]
        gain = g_ref[...] * (1.0 + mod_ref[0, 3 * sub + 1:3 * sub + 2, :])

        def norm_rows(r, carry):
            rows = pl.ds(pl.multiple_of(r * NORM_ROWS, NORM_ROWS), NORM_ROWS)
            x = x_ref[rows, :]
            o_ref[rows, :] = x
            inv = lax.rsqrt(jnp.mean(x * x, axis=-1, keepdims=True) + EPS)
            h_ref[rows, :] = (x * inv * gain + shift).astype(BF16)
            return carry

        lax.fori_loop(0, tm // NORM_ROWS, norm_rows, 0, unroll=NORM_UNROLL)

    gate = FFN_RES * mod_ref[0, 3 * sub + 2:3 * sub + 3, :]
    for r in range(tm // FFN_SUB_ROWS):
        rows = slice(r * FFN_SUB_ROWS, (r + 1) * FFN_SUB_ROWS)
        h = h_ref[rows, :]
        half = w1_ref.shape[1] // 2
        acts = []
        for p in range(2):
            ps = slice(p * half, (p + 1) * half)
            acts.append((_silu(_dot(h, w1_ref[:, ps])) * _dot(h, w3_ref[:, ps])).astype(BF16))
        for n in range(d_out // OUT_CHUNK):
            cs = slice(n * OUT_CHUNK, (n + 1) * OUT_CHUNK)
            o_ref[rows, cs] += gate[:, cs] * (_dot(acts[0], w2_ref[0:half, cs])
                                              + _dot(acts[1], w2_ref[half:2 * half, cs]))


def _ffn_call(x2, mod_l, g, w13, w2, sub, seq, casts=()):
    m, d = x2.shape
    d_ff = w2.shape[0]
    tiles_per_batch = seq // FFN_TM
    n_ff = d_ff // FFN_TF
    n_steps = (m // FFN_TM) * n_ff
    side_in, side_out, side_shapes = _cast_specs(casts, n_steps, lambda i, j: i * n_ff + j)
    vmem = (2 * 2 * FFN_TM * d * 4 + FFN_TM * d * 2 + 2 * 3 * d * FFN_TF * 2
            + FFN_SUB_ROWS * (3 * FFN_TF + OUT_CHUNK) * 4 + _cast_bytes(casts, n_steps)
            + VMEM_SLACK)
    outs = pl.pallas_call(
        functools.partial(_ffn_kernel, sub=sub, n_cast=len(casts)),
        out_shape=[jax.ShapeDtypeStruct((m, d), F32)] + side_shapes,
        grid=(m // FFN_TM, n_ff),
        in_specs=[
            pl.BlockSpec((FFN_TM, d), lambda i, j: (i, 0)),
            pl.BlockSpec((1, 9, d), lambda i, j: (i // tiles_per_batch, 0, 0)),
            pl.BlockSpec((1, d), lambda i, j: (0, 0)),
            pl.BlockSpec((d, FFN_TF), lambda i, j: (0, j)),
            pl.BlockSpec((d, FFN_TF), lambda i, j: (0, j + n_ff)),
            pl.BlockSpec((FFN_TF, d), lambda i, j: (j, 0)),
        ] + side_in,
        out_specs=[pl.BlockSpec((FFN_TM, d), lambda i, j: (i, 0))] + side_out,
        scratch_shapes=[pltpu.VMEM((FFN_TM, d), BF16)],
        compiler_params=pltpu.CompilerParams(
            dimension_semantics=("arbitrary", "arbitrary"),
            vmem_limit_bytes=vmem),
        name=f"ffn{sub}",
    )(x2, mod_l, g, w13, w13, w2, *[w for w, _ in casts])
    return outs[0], outs[1:]


def _in_mix_kernel(x_ref, mod_ref, g_ref, win_ref, poolw_ref, pools_ref, sguw_ref,
                   sgubt_ref, sgug_ref, gq_ref, gk_ref, convw_ref, ong_ref, *refs,
                   tiles_per_batch, n_cast):
    cast_src, refs = refs[:n_cast], refs[n_cast:]
    yabd_ref, q_ref, k_ref, vt_ref, kmean_ref = refs[:5]
    cast_dst, (pbuf, zbuf) = refs[5:5 + n_cast], refs[5 + n_cast:]
    _run_casts(cast_src, cast_dst)
    tm = x_ref.shape[0]
    tib = pl.program_id(0) % tiles_per_batch

    @pl.when(tib == 0)
    def _():
        pbuf[0:POOL_HALO, :] = jnp.zeros((POOL_HALO, GROUP_W), F32)
        zbuf[0:CONV_HALO, :] = jnp.zeros((CONV_HALO, GROUP_W), F32)

    shift = mod_ref[0, 3:4, :]
    scale = mod_ref[0, 4:5, :]
    h = (_rms(x_ref[...], g_ref[...]) * (1.0 + scale) + shift).astype(BF16)

    proj = {}

    def project(*groups):
        for g in groups:
            proj[g] = _dot(h, win_ref[:, g * GROUP_W:(g + 1) * GROUP_W])

    def heads():
        return [slice(hd * HEAD_DIM, (hd + 1) * HEAD_DIM) for hd in range(N_HEADS)]

    def pool_mixer():
        pbuf[POOL_HALO:POOL_HALO + tm, :] = proj[0]
        pos = tib * tm + lax.broadcasted_iota(jnp.int32, (tm, 1), 0)
        ya_parts = []
        for hd, (cs, win) in enumerate(zip(heads(), POOL_WINDOWS)):
            xg = pbuf[POOL_HALO:POOL_HALO + tm, cs]
            wsum = xg
            for lag in range(1, win):
                wsum = wsum + pbuf[POOL_HALO - lag:POOL_HALO - lag + tm, cs]
            cnt = jnp.minimum(pos + 1, win).astype(F32)
            dlt = wsum / cnt - xg
            ya_parts.append(_dot(dlt.astype(BF16), poolw_ref[hd].astype(BF16)))
        ya = jnp.concatenate(ya_parts, axis=1) * pools_ref[...]
        pbuf[0:POOL_HALO, :] = pbuf[tm:tm + POOL_HALO, :]
        yabd_ref[:, 0:GROUP_W] = _rms(ya, ong_ref[0:1, :]).astype(BF16)

    def sgu_mixer():
        u = _gelu_tanh(proj[1])
        v = _gelu_tanh(proj[2])
        r_t = lax.broadcasted_iota(jnp.int32, (SGU_CHUNK, SGU_CHUNK), 0)
        r_s = lax.broadcasted_iota(jnp.int32, (SGU_CHUNK, SGU_CHUNK), 1)
        yb_parts = []
        for hd, cs in enumerate(heads()):
            vh = _rms(v[:, cs], sgug_ref[:, cs]).astype(BF16)
            wm = jnp.where(r_t >= r_s, sguw_ref[hd], 0.0).astype(BF16)
            bias = sgubt_ref[:, hd:hd + 1]
            mixed = [_dot(wm, vh[c * SGU_CHUNK:(c + 1) * SGU_CHUNK, :]) + bias
                     for c in range(tm // SGU_CHUNK)]
            yb_parts.append(u[:, cs] * jnp.concatenate(mixed, axis=0))
        yb = jnp.concatenate(yb_parts, axis=1)
        yabd_ref[:, GROUP_W:2 * GROUP_W] = _rms(yb, ong_ref[1:2, :]).astype(BF16)

    def conv_mixer():
        zbuf[CONV_HALO:CONV_HALO + tm, :] = proj[7] * proj[8]
        conv = jnp.zeros((tm, GROUP_W), F32)
        for tap in range(CONV_WIDTH):
            off = CONV_HALO - (CONV_WIDTH - 1) + tap
            conv = conv + convw_ref[tap:tap + 1, :] * zbuf[off:off + tm, :]
        zbuf[0:CONV_HALO, :] = zbuf[tm:tm + CONV_HALO, :]
        yabd_ref[:, 2 * GROUP_W:3 * GROUP_W] = _rms(proj[6] * conv, ong_ref[3:4, :]).astype(BF16)

    def qk_prep():
        qk_scale = HEAD_DIM ** -0.5
        q = jnp.concatenate([_rms(proj[3][:, cs], gq_ref[...]) * qk_scale for cs in heads()],
                            axis=1)
        k = jnp.concatenate([_rms(proj[4][:, cs], gk_ref[...]) for cs in heads()], axis=1)
        q_ref[...] = q
        k_ref[...] = k.astype(BF16)
        kmean_ref[0] = jnp.mean(k, axis=0, keepdims=True)

    project(1, 2, 0, 3, 4)
    sgu_mixer()
    pool_mixer()
    qk_prep()
    project(6, 7, 8, 5)
    conv_mixer()
    vt_ref[0] = proj[5].T.astype(BF16)


def _in_mix_call(x2, mod_l, g, w_in, pool_w, pool_scale, sgu_w, sgu_bt, sgu_g,
                 gq, gk, conv_w, out_g, casts, seq):
    m, d = x2.shape
    tm = MOBA_BLOCK
    n_tiles = m // tm
    tiles_per_batch = seq // tm
    const2 = lambda i: (0, 0)
    const3 = lambda i: (0, 0, 0)
    cast_in, cast_out, cast_shapes = _cast_specs(casts, n_tiles, lambda i: i)
    vmem = (w_in.size * 2 + 2 * tm * d * 4 + 2 * tm * 6 * GROUP_W * 4
            + 2 * N_GROUPS_IN * tm * GROUP_W * 4 + _cast_bytes(casts, n_tiles) + VMEM_SLACK)
    outs = pl.pallas_call(
        functools.partial(_in_mix_kernel, tiles_per_batch=tiles_per_batch, n_cast=len(casts)),
        out_shape=[
            jax.ShapeDtypeStruct((m, 3 * GROUP_W), BF16),
            jax.ShapeDtypeStruct((m, GROUP_W), F32),
            jax.ShapeDtypeStruct((m, GROUP_W), BF16),
            jax.ShapeDtypeStruct((n_tiles, GROUP_W, tm), BF16),
            jax.ShapeDtypeStruct((n_tiles, 1, GROUP_W), F32),
        ] + cast_shapes,
        grid=(n_tiles,),
        in_specs=[
            pl.BlockSpec((tm, d), lambda i: (i, 0)),
            pl.BlockSpec((1, 9, d), lambda i: (i // tiles_per_batch, 0, 0)),
            pl.BlockSpec((1, d), const2),
            pl.BlockSpec(w_in.shape, const2),
            pl.BlockSpec(pool_w.shape, const3),
            pl.BlockSpec((1, GROUP_W), const2),
            pl.BlockSpec(sgu_w.shape, const3),
            pl.BlockSpec(sgu_bt.shape, const2),
            pl.BlockSpec((1, GROUP_W), const2),
            pl.BlockSpec((1, HEAD_DIM), const2),
            pl.BlockSpec((1, HEAD_DIM), const2),
            pl.BlockSpec(conv_w.shape, const2),
            pl.BlockSpec(out_g.shape, const2),
        ] + cast_in,
        out_specs=[
            pl.BlockSpec((tm, 3 * GROUP_W), lambda i: (i, 0)),
            pl.BlockSpec((tm, GROUP_W), lambda i: (i, 0)),
            pl.BlockSpec((tm, GROUP_W), lambda i: (i, 0)),
            pl.BlockSpec((1, GROUP_W, tm), lambda i: (i, 0, 0)),
            pl.BlockSpec((1, 1, GROUP_W), lambda i: (i, 0, 0)),
        ] + cast_out,
        scratch_shapes=[
            pltpu.VMEM((POOL_HALO + tm, GROUP_W), F32),
            pltpu.VMEM((CONV_HALO + tm, GROUP_W), F32),
        ],
        compiler_params=pltpu.CompilerParams(
            dimension_semantics=("arbitrary",),
            vmem_limit_bytes=vmem),
        name="in_mix",
    )(x2, mod_l, g, w_in, pool_w, pool_scale, sgu_w, sgu_bt, sgu_g, gq, gk, conv_w, out_g,
      *[w for w, _ in casts])
    return outs[:5], outs[5:]


def _moba_out_kernel(q_ref, k_ref, vt_ref, kmean_ref, yabd_ref, x_ref, mod_ref, ong_ref,
                     wout_ref, o_ref, qb_ref, ycn_ref, projl_ref, *, n_tiles):
    t = pl.program_id(0)
    tq = q_ref.shape[0]
    n_blk = kmean_ref.shape[1]
    own = jnp.minimum(t, n_tiles - 1) % n_blk
    neg_inf = -jnp.inf
    heads = [slice(hd * HEAD_DIM, (hd + 1) * HEAD_DIM) for hd in range(N_HEADS)]

    @pl.when(t == 0)
    def _():
        ycn_ref[...] = jnp.zeros_like(ycn_ref)
        projl_ref[...] = jnp.zeros_like(projl_ref)

    def tile(n_visit):
        blk_id = lax.broadcasted_iota(jnp.int32, (n_blk, tq), 0).astype(F32)
        past = blk_id < own.astype(F32)
        key_i = lax.broadcasted_iota(jnp.int32, (tq, tq), 0)
        qry_i = lax.broadcasted_iota(jnp.int32, (tq, tq), 1)

        n_chunks = o_ref.shape[1] // PROJ_CHUNK
        chunks_issued = [0]

        def issue_proj_chunks(units_done):
            while chunks_issued[0] * (n_visit + 1) < units_done * n_chunks:
                cs = slice(chunks_issued[0] * PROJ_CHUNK, (chunks_issued[0] + 1) * PROJ_CHUNK)
                projl_ref[:, cs] = (
                    _dot(yabd_ref[:, 0:2 * GROUP_W], wout_ref[0:2 * GROUP_W, cs])
                    + _dot(yabd_ref[:, 2 * GROUP_W:3 * GROUP_W],
                           wout_ref[3 * GROUP_W:4 * GROUP_W, cs]))
                chunks_issued[0] += 1

        qb_ref[...] = q_ref[...].astype(BF16)
        blk_scores = [_dot_nt(kmean_ref[0, :, cs], q_ref[:, cs], precision=lax.Precision.HIGHEST)
                      for cs in heads]
        own_scores = [_dot_nt(k_ref[0, own, :, cs], qb_ref[:, cs]) for cs in heads]

        def block_scores(nb):
            return [_dot_nt(k_ref[0, nb, :, cs], qb_ref[:, cs]) for cs in heads]

        scores_next = block_scores(0) if n_visit else None

        prev_proj = projl_ref[...] + _dot(ycn_ref[...], wout_ref[2 * GROUP_W:3 * GROUP_W, :])
        o_ref[...] = x_ref[...] + mod_ref[0, 5:6, :] * prev_proj

        issue_proj_chunks(1)
        sel, m, l, acc = [], [], [], []
        for hd, cs in enumerate(heads):
            sb = jnp.where(past, blk_scores[hd], neg_inf)
            chosen = jnp.zeros((n_blk, tq), F32)
            for _ in range(min(MOBA_TOPK, n_blk)):
                top = jnp.max(sb, axis=0, keepdims=True)
                idx = jnp.min(jnp.where(sb == top, blk_id, float(n_blk)), axis=0, keepdims=True)
                hit = blk_id == idx
                chosen = jnp.where(hit, 1.0, chosen)
                sb = jnp.where(hit, neg_inf, sb)
            sel.append(jnp.where(past, chosen, 0.0))

            s = jnp.where(key_i <= qry_i, own_scores[hd], neg_inf)
            m.append(jnp.max(s, axis=0, keepdims=True))
            p = jnp.exp(s - m[hd])
            l.append(jnp.sum(p, axis=0, keepdims=True))
            acc.append(_dot(vt_ref[0, own, cs, :], p.astype(BF16)))

        for nb in range(n_visit):
            scores = scores_next
            if nb + 1 < n_visit:
                scores_next = block_scores(nb + 1)
            issue_proj_chunks(nb + 2)
            for hd, cs in enumerate(heads):
                s = jnp.where(sel[hd][nb:nb + 1, :] > 0.0, scores[hd], neg_inf)
                m_new = jnp.maximum(m[hd], jnp.max(s, axis=0, keepdims=True))
                alpha = jnp.exp(m[hd] - m_new)
                p = jnp.exp(s - m_new)
                l[hd] = alpha * l[hd] + jnp.sum(p, axis=0, keepdims=True)
                acc[hd] = alpha * acc[hd] + _dot(vt_ref[0, nb, cs, :], p.astype(BF16))
                m[hd] = m_new

        yc = jnp.concatenate([(acc[hd] / l[hd]).T for hd in range(N_HEADS)], axis=1)
        ycn_ref[...] = _rms(yc, ong_ref[2:3, :]).astype(BF16)

    step = n_blk // MOBA_VARIANTS
    for v in range(MOBA_VARIANTS):
        in_range = jnp.logical_and(own >= v * step, own < (v + 1) * step)
        pl.when(in_range)(functools.partial(tile, (v + 1) * step - 1))


def _moba_out_call(q, k4, vt4, kmean, yabd, x2, mod_l, out_g, w_out):
    m, d = x2.shape
    n_batch, n_blk, tq, _ = k4.shape
    n_tiles = n_batch * n_blk
    cur = lambda t: jnp.minimum(t, n_tiles - 1)
    prev = lambda t: jnp.maximum(t - 1, 0)
    vmem = (2 * 2 * n_blk * tq * GROUP_W * 2 + w_out.size * 2 + 2 * 2 * tq * d * 4
            + 2 * tq * GROUP_W * (4 + 3 * 2) + 3 * tq * d * 4 + 6 * N_HEADS * tq * tq * 4
            + VMEM_SLACK)
    return pl.pallas_call(
        functools.partial(_moba_out_kernel, n_tiles=n_tiles),
        out_shape=jax.ShapeDtypeStruct((m, d), F32),
        grid=(n_tiles + 1,),
        in_specs=[
            pl.BlockSpec((tq, GROUP_W), lambda t: (cur(t), 0)),
            pl.BlockSpec((1, n_blk, tq, GROUP_W), lambda t: (cur(t) // n_blk, 0, 0, 0)),
            pl.BlockSpec((1, n_blk, GROUP_W, tq), lambda t: (cur(t) // n_blk, 0, 0, 0)),
            pl.BlockSpec((1, n_blk, GROUP_W), lambda t: (cur(t) // n_blk, 0, 0)),
            pl.BlockSpec((tq, 3 * GROUP_W), lambda t: (cur(t), 0)),
            pl.BlockSpec((tq, d), lambda t: (prev(t), 0)),
            pl.BlockSpec((1, 9, d), lambda t: (prev(t) // n_blk, 0, 0)),
            pl.BlockSpec(out_g.shape, lambda t: (0, 0)),
            pl.BlockSpec(w_out.shape, lambda t: (0, 0)),
        ],
        out_specs=pl.BlockSpec((tq, d), lambda t: (prev(t), 0)),
        scratch_shapes=[
            pltpu.VMEM((tq, GROUP_W), BF16),
            pltpu.VMEM((tq, GROUP_W), BF16),
            pltpu.VMEM((tq, d), F32),
        ],
        compiler_params=pltpu.CompilerParams(
            dimension_semantics=("arbitrary",),
            vmem_limit_bytes=vmem),
        name="moba_out",
    )(q, k4, vt4, kmean, yabd, x2, mod_l, out_g, w_out)


def kernel(x, c, ada_w, ada_b, norm_g, ffn1_w13, ffn1_w2, w_in, pool_w, pool_scale, sgu_w,
           sgu_b, sgu_norm_g, q_norm_g, k_norm_g, conv_w, out_norm_g, w_out, ffn2_w13, ffn2_w2):
    n_batch, seq, d = x.shape
    n_layers = ada_w.shape[0]
    assert seq % MOBA_BLOCK == 0 and seq % FFN_TM == 0 and MOBA_BLOCK % SGU_CHUNK == 0
    assert n_batch <= MOD_ROWS and w_in.shape[2] == N_GROUPS_IN * GROUP_W
    n_blk = seq // MOBA_BLOCK

    c_pad = jnp.zeros((MOD_ROWS, d), F32).at[:n_batch].set(c)
    ada_b3 = ada_b.reshape(n_layers, 1, ada_b.shape[1])
    mod0, w13_a, w2_a = _ada_call(c_pad, ada_w, ada_b3, 0, [(ffn1_w13, 0), (ffn1_w2, 0)])
    mods = [mod0] + [_ada_call(c_pad, ada_w, ada_b3, l, [])[0] for l in range(1, n_layers)]
    w_in_l = None

    x2 = x.reshape(n_batch * seq, d)
    for l in range(n_layers):
        mod_l = mods[l][:n_batch].reshape(n_batch, 9, d)
        x2, cast_weights = _ffn_call(x2, mod_l, norm_g[l, 0:1], w13_a, w2_a, 0, seq,
                                     [(w_in, l)] if w_in_l is None else [])
        if cast_weights:
            w_in_l, = cast_weights
        (yabd, q, k, vt, kmean), (w_out_l, w13_b, w2_b) = _in_mix_call(
            x2, mod_l, norm_g[l, 1:2], w_in_l, pool_w[l],
            pool_scale[l].reshape(1, GROUP_W), sgu_w[l], sgu_b[l].T,
            sgu_norm_g[l].reshape(1, GROUP_W), q_norm_g[l].reshape(1, HEAD_DIM),
            k_norm_g[l].reshape(1, HEAD_DIM), conv_w[l],
            out_norm_g[l].reshape(4, GROUP_W),
            [(w_out, l), (ffn2_w13, l), (ffn2_w2, l)], seq)
        x2 = _moba_out_call(
            q, k.reshape(n_batch, n_blk, MOBA_BLOCK, GROUP_W),
            vt.reshape(n_batch, n_blk, GROUP_W, MOBA_BLOCK),
            kmean.reshape(n_batch, n_blk, GROUP_W), yabd, x2, mod_l,
            out_norm_g[l].reshape(4, GROUP_W), w_out_l)
        next_casts = ([(ffn1_w13, l + 1), (ffn1_w2, l + 1), (w_in, l + 1)]
                      if l + 1 < n_layers else [])
        x2, cast_weights = _ffn_call(x2, mod_l, norm_g[l, 2:3], w13_b, w2_b, 2, seq, next_casts)
        if cast_weights:
            w13_a, w2_a, w_in_l = cast_weights
    return x2.reshape(n_batch, seq, d)
```
